```python
import math
import jax, jax.numpy as jnp
from jax import lax
import numpy as np

D_MODEL = 1024
BATCH = 8
SEQ = 4096
DEPTH = 2

N_HEADS = 8
QK_NOPE_DIM = 64
QK_ROPE_DIM = 32
V_HEAD_DIM = 64
Q_LORA_RANK = 384
KV_LORA_RANK = 256
ROPE_THETA = 10000.0
Q_BLOCK = 128
CONV_CHANNELS = 512
CONV_WIDTH = 31
POOL_WINDOWS = (2, 4, 8, 16)
POOL_GROUPS = 4
POOL_CHANNELS = 512
POOL_GROUP_DIM = POOL_CHANNELS // POOL_GROUPS
N_BRANCHES = 3
D_FF = -(-8 * D_MODEL // (3 * 256)) * 256
EPS = 1e-6
IN_WIDTHS = (Q_LORA_RANK, KV_LORA_RANK, QK_ROPE_DIM, 2 * CONV_CHANNELS, POOL_CHANNELS, N_BRANCHES * D_MODEL)
D_IN = sum(IN_WIDTHS)

kernel_name = "hybrid_mla_conformer_pool_gated_block"


def rms_norm(x, g):
    x32 = x.astype(jnp.float32)
    y = x32 * lax.rsqrt(jnp.mean(x32 * x32, axis=-1, keepdims=True) + EPS)
    return (y * g.astype(jnp.float32)).astype(x.dtype)


def layer_norm(x, g, b):
    x32 = x.astype(jnp.float32)
    mu = jnp.mean(x32, axis=-1, keepdims=True)
    xc = x32 - mu
    y = xc * lax.rsqrt(jnp.mean(xc * xc, axis=-1, keepdims=True) + EPS)
    return (y * g.astype(jnp.float32) + b.astype(jnp.float32)).astype(x.dtype)


def rope_tables(positions):
    inv_freq = ROPE_THETA ** (-jnp.arange(0, QK_ROPE_DIM, 2, dtype=jnp.float32) / QK_ROPE_DIM)
    ang = positions.astype(jnp.float32)[..., None] * inv_freq
    return jnp.cos(ang), jnp.sin(ang)


def apply_rope(x, cos, sin):
    x32 = x.astype(jnp.float32)
    half = x32.shape[-1] // 2
    x1, x2 = x32[..., :half], x32[..., half:]
    out = jnp.concatenate([x1 * cos - x2 * sin, x2 * cos + x1 * sin], axis=-1)
    return out.astype(x.dtype)


def mla_branch(c_q, c_kv, k_rope_raw, cos, sin, q_norm, w_uq, kv_norm, w_uk, w_uv, w_o):
    B, S, _ = c_q.shape
    q = (rms_norm(c_q, q_norm) @ w_uq).reshape(B, S, N_HEADS, QK_NOPE_DIM + QK_ROPE_DIM)
    q_nope = q[..., :QK_NOPE_DIM]
    q_rope = apply_rope(q[..., QK_NOPE_DIM:], cos[:, :, None, :], sin[:, :, None, :])
    c_kv_n = rms_norm(c_kv, kv_norm)
    k_nope = (c_kv_n @ w_uk).reshape(B, S, N_HEADS, QK_NOPE_DIM)
    v = (c_kv_n @ w_uv).reshape(B, S, N_HEADS, V_HEAD_DIM)
    k_rope = apply_rope(k_rope_raw, cos, sin)
    nb = S // Q_BLOCK
    qn_blocks = q_nope.reshape(B, nb, Q_BLOCK, N_HEADS, QK_NOPE_DIM).transpose(1, 0, 2, 3, 4)
    qr_blocks = q_rope.reshape(B, nb, Q_BLOCK, N_HEADS, QK_ROPE_DIM).transpose(1, 0, 2, 3, 4)
    starts = jnp.arange(nb, dtype=jnp.int32) * Q_BLOCK
    key_idx = jnp.arange(S, dtype=jnp.int32)
    scale = 1.0 / math.sqrt(QK_NOPE_DIM + QK_ROPE_DIM)

    def attend(args):
        qn, qr, start = args
        s = jnp.einsum('bqhd,bkhd->bhqk', qn, k_nope) + jnp.einsum('bqhr,bkr->bhqk', qr, k_rope)
        s = s.astype(jnp.float32) * scale
        q_idx = start + jnp.arange(Q_BLOCK, dtype=jnp.int32)
        causal = key_idx[None, :] <= q_idx[:, None]
        s = jnp.where(causal[None, None], s, -jnp.inf)
        p = jax.nn.softmax(s, axis=-1).astype(v.dtype)
        return jnp.einsum('bhqk,bkhd->bqhd', p, v)

    o = lax.map(attend, (qn_blocks, qr_blocks, starts))
    o = o.transpose(1, 0, 2, 3, 4).reshape(B, S, N_HEADS * V_HEAD_DIM)
    return o @ w_o


def conv_branch(u, conv_w, conv_b, ln_g, ln_b, w_out):
    a, g = jnp.split(u, 2, axis=-1)
    h = a * jax.nn.sigmoid(g)
    h = lax.conv_general_dilated(h, conv_w[:, None, :], window_strides=(1,),
                                 padding=[(CONV_WIDTH - 1, 0)],
                                 dimension_numbers=('NWC', 'WIO', 'NWC'),
                                 feature_group_count=CONV_CHANNELS) + conv_b
    h = jax.nn.silu(layer_norm(h, ln_g, ln_b))
    return h @ w_out


def pool_branch(u, pool_w, pool_scale, w_out):
    B, S, _ = u.shape
    u32 = u.astype(jnp.float32).reshape(B, S, POOL_GROUPS, POOL_GROUP_DIM)
    cs = jnp.cumsum(u32, axis=1)
    t = jnp.arange(S, dtype=jnp.int32)
    outs = []
    for gi, w in enumerate(POOL_WINDOWS):
        c = cs[:, :, gi]
        lag = jnp.pad(c, ((0, 0), (w, 0), (0, 0)))[:, :S]
        count = jnp.minimum(t + 1, w).astype(jnp.float32)[None, :, None]
        outs.append((c - lag) / count)
    pooled = jnp.stack(outs, axis=2)
    d = (pooled - u32).astype(u.dtype)
    m = jnp.einsum('bsgc,gcd->bsgd', d, pool_w).reshape(B, S, POOL_CHANNELS) * pool_scale
    return m @ w_out


def mixer_sublayer(x, cos, sin, norm_pre, w_in, q_norm, w_uq, kv_norm, w_uk, w_uv, w_attn_o,
                   conv_w, conv_b, conv_ln_g, conv_ln_b, w_conv_o, pool_w, pool_scale, w_pool_o,
                   w_mix_o, norm_post):
    B, S, D = x.shape
    h = rms_norm(x, norm_pre)
    z = h @ w_in
    split_at = [int(v) for v in np.cumsum(IN_WIDTHS)[:-1]]
    c_q, c_kv, k_r, u_conv, u_pool, gate_logits = jnp.split(z, split_at, axis=-1)
    y_attn = mla_branch(c_q, c_kv, k_r, cos, sin, q_norm, w_uq, kv_norm, w_uk, w_uv, w_attn_o)
    y_conv = conv_branch(u_conv, conv_w, conv_b, conv_ln_g, conv_ln_b, w_conv_o)
    y_pool = pool_branch(u_pool, pool_w, pool_scale, w_pool_o)
    gates = jax.nn.sigmoid(gate_logits.astype(jnp.float32)).astype(x.dtype).reshape(B, S, N_BRANCHES, D)
    merged = gates[:, :, 0] * y_attn + gates[:, :, 1] * y_conv + gates[:, :, 2] * y_pool
    return rms_norm(merged @ w_mix_o, norm_post)


def ffn_sublayer(x, norm_pre, w_gate, w_up, w_down, norm_post):
    h = rms_norm(x, norm_pre)
    y = (jax.nn.silu(h @ w_gate) * (h @ w_up)) @ w_down
    return rms_norm(y, norm_post)


def setup_inputs(seed: int = 0) -> dict:
    key = jax.random.key(seed)
    ks = jax.random.split(key, 32)
    L, D = DEPTH, D_MODEL

    def dense(k, shape, fan_in):
        return jax.random.normal(k, shape, jnp.float32) * fan_in ** -0.5

    def gain(k, shape):
        return 1.0 + 0.02 * jax.random.normal(k, shape, jnp.float32)

    x = jax.random.normal(ks[0], (BATCH, SEQ, D), jnp.float32)
    offsets = jax.random.randint(ks[1], (BATCH, 1), 0, 4096, dtype=jnp.int32)
    positions = offsets + jnp.arange(SEQ, dtype=jnp.int32)[None, :]
    return {
        "x": x,
        "positions": positions,
        "mix_norm_pre": gain(ks[2], (L, D)),
        "w_in": dense(ks[3], (L, D, D_IN), D),
        "q_norm": gain(ks[4], (L, Q_LORA_RANK)),
        "w_uq": dense(ks[5], (L, Q_LORA_RANK, N_HEADS * (QK_NOPE_DIM + QK_ROPE_DIM)), Q_LORA_RANK),
        "kv_norm": gain(ks[6], (L, KV_LORA_RANK)),
        "w_uk": dense(ks[7], (L, KV_LORA_RANK, N_HEADS * QK_NOPE_DIM), KV_LORA_RANK),
        "w_uv": dense(ks[8], (L, KV_LORA_RANK, N_HEADS * V_HEAD_DIM), KV_LORA_RANK),
        "w_attn_o": dense(ks[9], (L, N_HEADS * V_HEAD_DIM, D), N_HEADS * V_HEAD_DIM),
        "conv_w": dense(ks[10], (L, CONV_WIDTH, CONV_CHANNELS), CONV_WIDTH),
        "conv_b": 0.02 * jax.random.normal(ks[11], (L, CONV_CHANNELS), jnp.float32),
        "conv_ln_g": gain(ks[12], (L, CONV_CHANNELS)),
        "conv_ln_b": 0.02 * jax.random.normal(ks[13], (L, CONV_CHANNELS), jnp.float32),
        "w_conv_o": dense(ks[14], (L, CONV_CHANNELS, D), CONV_CHANNELS),
        "pool_w": dense(ks[15], (L, POOL_GROUPS, POOL_GROUP_DIM, POOL_GROUP_DIM), POOL_GROUP_DIM),
        "pool_scale": 1.0 + 0.1 * jax.random.normal(ks[16], (L, POOL_CHANNELS), jnp.float32),
        "w_pool_o": dense(ks[17], (L, POOL_CHANNELS, D), POOL_CHANNELS),
        "w_mix_o": dense(ks[18], (L, D, D), D),
        "mix_norm_post": gain(ks[19], (L, D)),
        "ffn_norm_pre": gain(ks[20], (L, D)),
        "w_gate": dense(ks[21], (L, D, D_FF), D),
        "w_up": dense(ks[22], (L, D, D_FF), D),
        "w_down": dense(ks[23], (L, D_FF, D), D_FF),
        "ffn_norm_post": gain(ks[24], (L, D)),
    }


def reference(x, positions, mix_norm_pre, w_in, q_norm, w_uq, kv_norm, w_uk, w_uv, w_attn_o,
              conv_w, conv_b, conv_ln_g, conv_ln_b, w_conv_o, pool_w, pool_scale, w_pool_o,
              w_mix_o, mix_norm_post, ffn_norm_pre, w_gate, w_up, w_down, ffn_norm_post):
    cos, sin = rope_tables(positions)
    h = x
    for l in range(DEPTH):
        h = h + mixer_sublayer(h, cos, sin, mix_norm_pre[l], w_in[l], q_norm[l], w_uq[l], kv_norm[l],
                               w_uk[l], w_uv[l], w_attn_o[l], conv_w[l], conv_b[l], conv_ln_g[l],
                               conv_ln_b[l], w_conv_o[l], pool_w[l], pool_scale[l], w_pool_o[l],
                               w_mix_o[l], mix_norm_post[l])
        h = h + ffn_sublayer(h, ffn_norm_pre[l], w_gate[l], w_up[l], w_down[l], ffn_norm_post[l])
    return h
```

```python
import functools
import math

import jax
import jax.numpy as jnp
from jax import lax
from jax.experimental import pallas as pl
from jax.experimental.pallas import tpu as pltpu

D_MODEL = 1024
N_HEADS = 8
NOPE = 64
ROPE = 32
ROPE_HALF = ROPE // 2
V_DIM = 64
Q_RANK = 384
KV_RANK = 256
ROPE_THETA = 10000.0
CONV_CH = 512
CONV_W = 31
POOL_WINDOWS = (2, 4, 8, 16)
POOL_CH = 512
POOL_GD = POOL_CH // len(POOL_WINDOWS)
D_FF = 2816
EPS = 1e-6

LANES = 128
SUBLANES = 8
VMEM_LIMIT_BYTES = 56 * 1024 * 1024

HEAD_PAD = LANES
QK_DIM = NOPE + ROPE
Q_SCALE = (1.0 / math.sqrt(QK_DIM)) * math.log2(math.e)

SEG_Q = (0, 384)
SEG_KV = (384, 640)
SEG_KR = (640, 768)
SEG_CA = (768, 1280)
SEG_CG = (1280, 1792)
SEG_POOL = (1792, 2304)
SEG_G0 = (2304, 3328)
SEG_G1 = (3328, 4352)
SEG_G2 = (4352, 5376)
D_IN_PAD = SEG_G2[1]

CONV_HALO = 32
POOL_HALO = 16
CONV_ROWS = 32

F32 = jnp.float32
BF16 = jnp.bfloat16

_NN = (((1,), (0,)), ((), ()))
_NT = (((1,), (1,)), ((), ()))
_TN = (((0,), (0,)), ((), ()))


def _rms(x, g):
    return x * lax.rsqrt(jnp.mean(x * x, axis=-1, keepdims=True) + EPS) * g


def _dot(a, b, dims=_NN):
    return lax.dot_general(a, b, dims, preferred_element_type=F32)


def _rope_rows(x1, x2, cos, sin):
    return x1 * cos - x2 * sin, x2 * cos + x1 * sin


def _rope_table_kernel(pos_ref, invf_ref, cos_ref, sin_ref):
    ang = invf_ref[...] * pos_ref[0].astype(F32)
    cos_ref[0] = jnp.cos(ang)
    sin_ref[0] = jnp.sin(ang)


def _rope_tables(positions):
    b, s = positions.shape
    inv_freq = ROPE_THETA ** (-jnp.arange(0, ROPE, 2, dtype=F32) / ROPE)
    out = jax.ShapeDtypeStruct((b, ROPE_HALF, s), F32)
    return pl.pallas_call(
        _rope_table_kernel,
        grid=(b,),
        in_specs=[
            pl.BlockSpec((1, 1, s), lambda i: (i, 0, 0)),
            pl.BlockSpec((ROPE_HALF, 1), lambda i: (0, 0)),
        ],
        out_specs=[
            pl.BlockSpec((1, ROPE_HALF, s), lambda i: (i, 0, 0)),
            pl.BlockSpec((1, ROPE_HALF, s), lambda i: (i, 0, 0)),
        ],
        out_shape=[out, out],
        name="rope_tables",
    )(positions.reshape(b, 1, s), inv_freq.reshape(ROPE_HALF, 1))


def _mixer_in_kernel(x_ref, cos_ref, sin_ref, npre_ref, win_ref, qn_ref, wuqt_ref, kvn_ref, wuk_ref, wuvt_ref,
                     cw_ref, cb_ref, lng_ref, lnb_ref, wco_ref, pbd_ref, ps_ref, wpo_ref,
                     qt_ref, k_ref, vt_ref, g0_ref, part_ref,
                     cext_ref, conv_ref, pext_ref, *, tm):
    i = pl.program_id(1)
    hn = _rms(x_ref[0], npre_ref[...]).astype(BF16)

    def proj(seg):
        return _dot(hn, win_ref[:, seg[0]:seg[1]])

    cos = cos_ref[0]
    sin = sin_ref[0]

    cq = _rms(proj(SEG_Q), qn_ref[...]).astype(BF16)
    qt = _dot(wuqt_ref[...], cq, _NT)
    for h in range(N_HEADS):
        base = h * HEAD_PAD
        r1, r2 = _rope_rows(qt[base + NOPE:base + NOPE + ROPE_HALF], qt[base + NOPE + ROPE_HALF:base + QK_DIM],
                            cos, sin)
        blk = jnp.concatenate([qt[base:base + NOPE], r1, r2, qt[base + QK_DIM:base + HEAD_PAD]], axis=0)
        qt_ref[0, base:base + HEAD_PAD, :] = (blk * Q_SCALE).astype(BF16)

    ckv = _rms(proj(SEG_KV), kvn_ref[...]).astype(BF16)
    kpad = _dot(ckv, wuk_ref[...])
    vt_ref[0] = _dot(wuvt_ref[...], ckv, _NT).astype(BF16)
    krt = proj(SEG_KR).T
    r1, r2 = _rope_rows(krt[0:ROPE_HALF], krt[ROPE_HALF:ROPE], cos, sin)
    kr = jnp.concatenate([jnp.zeros((NOPE, tm), F32), r1, r2, jnp.zeros((HEAD_PAD - QK_DIM, tm), F32)], axis=0).T
    for h in range(N_HEADS):
        base = h * HEAD_PAD
        k_ref[0, :, base:base + HEAD_PAD] = (kpad[:, base:base + HEAD_PAD] + kr).astype(BF16)

    glu = proj(SEG_CA) * jax.nn.sigmoid(proj(SEG_CG))

    @pl.when(i == 0)
    def _():
        cext_ref[0:CONV_HALO, :] = jnp.zeros((CONV_HALO, CONV_CH), F32)
        pext_ref[0:POOL_HALO, :] = jnp.zeros((POOL_HALO, POOL_CH), F32)

    cext_ref[CONV_HALO:CONV_HALO + tm, :] = glu
    for r0 in range(0, tm, CONV_ROWS):
        acc = None
        for res in range(SUBLANES):
            shifts = [s for s in range(CONV_HALO - CONV_W + 1, CONV_HALO + 1) if s % SUBLANES == res]
            lo = min(shifts) - res
            hi = max(shifts) - res
            slab = cext_ref[r0 + lo + res:r0 + hi + res + CONV_ROWS, :]
            for s in shifts:
                j = s - (CONV_HALO - CONV_W + 1)
                off = s - res - lo
                term = slab[off:off + CONV_ROWS] * cw_ref[j:j + 1, :]
                acc = term if acc is None else acc + term
        conv_ref[r0:r0 + CONV_ROWS, :] = acc
    cext_ref[0:CONV_HALO, :] = cext_ref[tm:tm + CONV_HALO, :]

    hc = conv_ref[...] + cb_ref[...]
    xc = hc - jnp.mean(hc, axis=-1, keepdims=True)
    ln = xc * lax.rsqrt(jnp.mean(xc * xc, axis=-1, keepdims=True) + EPS) * lng_ref[...] + lnb_ref[...]
    yconv = _dot((ln * jax.nn.sigmoid(ln)).astype(BF16), wco_ref[...])

    u = proj(SEG_POOL)
    pext_ref[POOL_HALO:POOL_HALO + tm, :] = u
    t1 = i * tm + 1 + lax.broadcasted_iota(jnp.int32, (tm, 1), 0)
    diffs = []
    for gi, w in enumerate(POOL_WINDOWS):
        lanes = slice(gi * POOL_GD, (gi + 1) * POOL_GD)
        tot = u[:, lanes]
        for d in range(1, w):
            tot = tot + pext_ref[POOL_HALO - d:POOL_HALO - d + tm, lanes]
        cnt = jnp.minimum(t1, w).astype(F32)
        diffs.append(tot / cnt - u[:, lanes])
    pext_ref[0:POOL_HALO, :] = pext_ref[tm:tm + POOL_HALO, :]
    dmix = jnp.concatenate(diffs, axis=1).astype(BF16)
    mixed = _dot(dmix, pbd_ref[...]) * ps_ref[...]
    ypool = _dot(mixed.astype(BF16), wpo_ref[...])

    part = jax.nn.sigmoid(proj(SEG_G1)) * yconv + jax.nn.sigmoid(proj(SEG_G2)) * ypool
    part_ref[0] = part.astype(BF16)
    g0_ref[0] = jax.nn.sigmoid(proj(SEG_G0)).astype(BF16)


def _const_spec(shape):
    return pl.BlockSpec(shape, lambda b, i: (0,) * len(shape), pipeline_mode=pl.Buffered(1))


def _mixer_in(x, cos_t, sin_t, p, *, tm):
    b, s, d = x.shape
    nt = s // tm
    row_tile = lambda w: pl.BlockSpec((1, tm, w), lambda bi, i: (bi, i, 0))
    col_tile = lambda r: pl.BlockSpec((1, r, tm), lambda bi, i: (bi, 0, i))
    consts = [p["npre"], p["w_in"], p["q_norm"], p["w_uq_t"], p["kv_norm"], p["w_uk"], p["w_uv_t"],
              p["conv_w"], p["conv_b"], p["ln_g"], p["ln_b"], p["w_conv_o"], p["pool_bd"], p["pool_scale"],
              p["w_pool_o"]]
    return pl.pallas_call(
        functools.partial(_mixer_in_kernel, tm=tm),
        grid=(b, nt),
        in_specs=[row_tile(d), col_tile(ROPE_HALF), col_tile(ROPE_HALF)] + [_const_spec(c.shape) for c in consts],
        out_specs=[col_tile(N_HEADS * HEAD_PAD), row_tile(N_HEADS * HEAD_PAD), col_tile(N_HEADS * V_DIM),
                   row_tile(d), row_tile(d)],
        out_shape=[
            jax.ShapeDtypeStruct((b, N_HEADS * HEAD_PAD, s), BF16),
            jax.ShapeDtypeStruct((b, s, N_HEADS * HEAD_PAD), BF16),
            jax.ShapeDtypeStruct((b, N_HEADS * V_DIM, s), BF16),
            jax.ShapeDtypeStruct((b, s, d), BF16),
            jax.ShapeDtypeStruct((b, s, d), BF16),
        ],
        scratch_shapes=[
            pltpu.VMEM((tm + CONV_HALO, CONV_CH), F32),
            pltpu.VMEM((tm, CONV_CH), F32),
            pltpu.VMEM((tm + POOL_HALO, POOL_CH), F32),
        ],
        compiler_params=pltpu.CompilerParams(
            dimension_semantics=("arbitrary", "arbitrary"), vmem_limit_bytes=VMEM_LIMIT_BYTES),
        name="mixer_in",
    )(x, cos_t, sin_t, *consts)


def _attn_kernel(qt_ref, k_ref, vt_ref, ot_ref, *, tq, tk):
    i = pl.program_id(1)

    def chunk(c, carry, masked):
        k0 = pl.multiple_of(c * tk, tk)
        new = []
        for h in range(N_HEADS):
            m, l, acc = carry[h]
            kc = k_ref[0, pl.ds(k0, tk), h * HEAD_PAD:(h + 1) * HEAD_PAD]
            s = _dot(kc, qt_ref[0, h * HEAD_PAD:(h + 1) * HEAD_PAD, :])
            if masked:
                key = lax.broadcasted_iota(jnp.int32, (tk, 1), 0)
                qry = lax.broadcasted_iota(jnp.int32, (1, tq), 1)
                s = jnp.where(key <= qry, s, -jnp.inf)
            m_new = jnp.maximum(m, jnp.max(s, axis=0, keepdims=True))
            alpha = jnp.exp2(m - m_new)
            pr = jnp.exp2(s - m_new)
            l_new = alpha * l + jnp.sum(pr, axis=0, keepdims=True)
            vc = vt_ref[0, h * V_DIM:(h + 1) * V_DIM, pl.ds(k0, tk)]
            acc_new = alpha * acc + _dot(vc, pr.astype(BF16))
            new.append((m_new, l_new, acc_new))
        return tuple(new)

    init = tuple((jnp.full((1, tq), -jnp.inf, F32), jnp.zeros((1, tq), F32), jnp.zeros((V_DIM, tq), F32))
                 for _ in range(N_HEADS))
    carry = lax.fori_loop(0, i, lambda c, cr: chunk(c, cr, False), init)
    carry = chunk(i, carry, True)
    for h in range(N_HEADS):
        _, l, acc = carry[h]
        ot_ref[0, h * V_DIM:(h + 1) * V_DIM, :] = (acc / l).astype(BF16)


def _attention(q_t, k, v_t, *, tq):
    b, _, s = q_t.shape
    return pl.pallas_call(
        functools.partial(_attn_kernel, tq=tq, tk=tq),
        grid=(b, s // tq),
        in_specs=[
            pl.BlockSpec((1, N_HEADS * HEAD_PAD, tq), lambda bi, i: (bi, 0, i)),
            pl.BlockSpec((1, s, N_HEADS * HEAD_PAD), lambda bi, i: (bi, 0, 0), pipeline_mode=pl.Buffered(1)),
            pl.BlockSpec((1, N_HEADS * V_DIM, s), lambda bi, i: (bi, 0, 0), pipeline_mode=pl.Buffered(1)),
        ],
        out_specs=pl.BlockSpec((1, N_HEADS * V_DIM, tq), lambda bi, i: (bi, 0, i)),
        out_shape=jax.ShapeDtypeStruct((b, N_HEADS * V_DIM, s), BF16),
        compiler_params=pltpu.CompilerParams(
            dimension_semantics=("arbitrary", "arbitrary"), vmem_limit_bytes=VMEM_LIMIT_BYTES),
        name="attention",
    )(q_t, k, v_t)


def _merge_ffn_kernel(h_ref, ot_ref, g0_ref, part_ref, wao_ref, wmo_ref, npost_ref, fpre_ref, wg_ref, wu_ref,
                      wd_ref, fpost_ref, out_ref):
    yattn = _dot(ot_ref[0], wao_ref[...], _TN)
    merged = g0_ref[0].astype(F32) * yattn + part_ref[0].astype(F32)
    h1 = h_ref[0] + _rms(_dot(merged.astype(BF16), wmo_ref[...]), npost_ref[...])
    hn = _rms(h1, fpre_ref[...]).astype(BF16)
    gate = _dot(hn, wg_ref[...])
    act = (gate * jax.nn.sigmoid(gate) * _dot(hn, wu_ref[...])).astype(BF16)
    out_ref[0] = h1 + _rms(_dot(act, wd_ref[...]), fpost_ref[...])


def _merge_ffn(h, o_t, g0, part, p, *, tm):
    b, s, d = h.shape
    row_tile = lambda w: pl.BlockSpec((1, tm, w), lambda bi, i: (bi, i, 0))
    consts = [p["w_attn_o"], p["w_mix_o"], p["npost"], p["fpre"], p["w_gate"], p["w_up"], p["w_down"], p["fpost"]]
    return pl.pallas_call(
        _merge_ffn_kernel,
        grid=(b, s // tm),
        in_specs=[row_tile(d), pl.BlockSpec((1, N_HEADS * V_DIM, tm), lambda bi, i: (bi, 0, i)), row_tile(d),
                  row_tile(d)] + [_const_spec(c.shape) for c in consts],
        out_specs=row_tile(d),
        out_shape=jax.ShapeDtypeStruct((b, s, d), F32),
        compiler_params=pltpu.CompilerParams(
            dimension_semantics=("arbitrary", "arbitrary"), vmem_limit_bytes=VMEM_LIMIT_BYTES),
        name="merge_ffn",
    )(h, o_t, g0, part, *consts)


def _layer_params(l, mix_norm_pre, w_in, q_norm, w_uq, kv_norm, w_uk, w_uv, w_attn_o, conv_w, conv_b, conv_ln_g,
                  conv_ln_b, w_conv_o, pool_w, pool_scale, w_pool_o, w_mix_o, mix_norm_post, ffn_norm_pre, w_gate,
                  w_up, w_down, ffn_norm_post):
    row = lambda v: v[l].reshape(1, -1)
    wi = w_in[l]
    kr_lo, kr_hi = SEG_KR[0], SEG_KR[0] + ROPE
    w_in_p = jnp.concatenate(
        [wi[:, :kr_hi], jnp.zeros((D_MODEL, SEG_KR[1] - kr_hi), wi.dtype), wi[:, kr_hi:]], axis=1)
    assert w_in_p.shape[1] == D_IN_PAD and kr_lo == KV_RANK + Q_RANK
    pad_heads = lambda w, dh: jnp.pad(w.reshape(w.shape[0], N_HEADS, dh),
                                      ((0, 0), (0, 0), (0, HEAD_PAD - dh))).reshape(w.shape[0], N_HEADS * HEAD_PAD)
    pool_bd = jnp.zeros((POOL_CH, POOL_CH), F32)
    for gi in range(len(POOL_WINDOWS)):
        sl = slice(gi * POOL_GD, (gi + 1) * POOL_GD)
        pool_bd = pool_bd.at[sl, sl].set(pool_w[l, gi])
    return dict(
        npre=row(mix_norm_pre), w_in=w_in_p.astype(BF16), q_norm=row(q_norm),
        w_uq_t=pad_heads(w_uq[l], QK_DIM).T.astype(BF16), kv_norm=row(kv_norm),
        w_uk=pad_heads(w_uk[l], NOPE).astype(BF16), w_uv_t=w_uv[l].T.astype(BF16),
        conv_w=conv_w[l], conv_b=row(conv_b), ln_g=row(conv_ln_g), ln_b=row(conv_ln_b),
        w_conv_o=w_conv_o[l].astype(BF16), pool_bd=pool_bd.astype(BF16), pool_scale=row(pool_scale),
        w_pool_o=w_pool_o[l].astype(BF16),
        w_attn_o=w_attn_o[l].astype(BF16), w_mix_o=w_mix_o[l].astype(BF16), npost=row(mix_norm_post),
        fpre=row(ffn_norm_pre), w_gate=w_gate[l].astype(BF16), w_up=w_up[l].astype(BF16),
        w_down=w_down[l].astype(BF16), fpost=row(ffn_norm_post),
    )


def kernel(x, positions, mix_norm_pre, w_in, q_norm, w_uq, kv_norm, w_uk, w_uv, w_attn_o, conv_w, conv_b, conv_ln_g, conv_ln_b, w_conv_o, pool_w, pool_scale, w_pool_o, w_mix_o, mix_norm_post, ffn_norm_pre, w_gate, w_up, w_down, ffn_norm_post):
    weights = (mix_norm_pre, w_in, q_norm, w_uq, kv_norm, w_uk, w_uv, w_attn_o, conv_w, conv_b, conv_ln_g,
               conv_ln_b, w_conv_o, pool_w, pool_scale, w_pool_o, w_mix_o, mix_norm_post, ffn_norm_pre, w_gate,
               w_up, w_down, ffn_norm_post)
    s = x.shape[1]
    tm = min(256, s)
    cos_t, sin_t = _rope_tables(positions)
    h = x
    for l in range(mix_norm_pre.shape[0]):
        p = _layer_params(l, *weights)
        q_t, k, v_t, g0, part = _mixer_in(h, cos_t, sin_t, p, tm=tm)
        o_t = _attention(q_t, k, v_t, tq=min(256, s))
        h = _merge_ffn(h, o_t, g0, part, p, tm=tm)
    return h
```

```python
import functools
import math

import jax
import jax.numpy as jnp
from jax import lax
from jax.experimental import pallas as pl
from jax.experimental.pallas import tpu as pltpu

D_MODEL = 1024
N_HEADS = 8
NOPE = 64
ROPE = 32
ROPE_HALF = ROPE // 2
V_DIM = 64
Q_RANK = 384
KV_RANK = 256
ROPE_THETA = 10000.0
CONV_CH = 512
CONV_W = 31
POOL_WINDOWS = (2, 4, 8, 16)
POOL_CH = 512
POOL_GD = POOL_CH // len(POOL_WINDOWS)
N_BRANCHES = 3
D_FF = 2816
EPS = 1e-6

LANES = 128
SUBLANES = 8
BF16_ROWS = 16
MXU_DIM = 256
VMEM_LIMIT_BYTES = 56 * 1024 * 1024

HEAD_PAD = LANES
QK_DIM = NOPE + ROPE
V_ROWS = V_DIM + BF16_ROWS
Q_SCALE = (1.0 / math.sqrt(QK_DIM)) * math.log2(math.e)

SEG_Q = (0, 384)
SEG_KV = (384, 640)
SEG_KR = (640, 768)
SEG_CA = (768, 1280)
SEG_CG = (1280, 1792)
SEG_POOL = (1792, 2304)
SEG_GATES = (2304, 5376)
D_IN_PAD = SEG_GATES[1]

CONV_HALO = 32
POOL_HALO = 16
CONV_ROWS = 32

ATTN_TQ = 512
ATTN_TK = 256

F32 = jnp.float32
BF16 = jnp.bfloat16

_NN = (((1,), (0,)), ((), ()))
_NT = (((1,), (1,)), ((), ()))
_TN = (((0,), (0,)), ((), ()))


def _rms(x, g):
    return x * lax.rsqrt(jnp.mean(x * x, axis=-1, keepdims=True) + EPS) * g


def _dot(a, b, dims=_NN):
    return lax.dot_general(a, b, dims, preferred_element_type=F32)


def _rope_rows(x1, x2, cos, sin):
    return x1 * cos - x2 * sin, x2 * cos + x1 * sin


def _rope_table_kernel(pos_ref, invf_ref, cos_ref, sin_ref):
    ang = invf_ref[...] * pos_ref[0].astype(F32)
    cos_ref[0] = jnp.cos(ang)
    sin_ref[0] = jnp.sin(ang)


def _rope_tables(positions):
    b, s = positions.shape
    inv_freq = ROPE_THETA ** (-jnp.arange(0, ROPE, 2, dtype=F32) / ROPE)
    out = jax.ShapeDtypeStruct((b, ROPE_HALF, s), F32)
    return pl.pallas_call(
        _rope_table_kernel,
        grid=(b,),
        in_specs=[
            pl.BlockSpec((1, 1, s), lambda i: (i, 0, 0)),
            pl.BlockSpec((ROPE_HALF, 1), lambda i: (0, 0)),
        ],
        out_specs=[
            pl.BlockSpec((1, ROPE_HALF, s), lambda i: (i, 0, 0)),
            pl.BlockSpec((1, ROPE_HALF, s), lambda i: (i, 0, 0)),
        ],
        out_shape=[out, out],
        name="rope_tables",
    )(positions.reshape(b, 1, s), inv_freq.reshape(ROPE_HALF, 1))


def _mixer_in_kernel(x_ref, cos_ref, sin_ref, npre_ref, win_ref, qn_ref, wuqt_ref, kvn_ref, wuk_ref, wuvt_ref,
                     cw_ref, cb_ref, lng_ref, lnb_ref, wco_ref, pbd_ref, ps_ref, wpo_ref,
                     qt_ref, k_ref, vt_ref, g0_ref, part_ref,
                     cext_ref, cshift_ref, conv_ref, pext_ref, gates_ref, *, tm):
    i = pl.program_id(1)
    hn = _rms(x_ref[0], npre_ref[...]).astype(BF16)

    def proj(lo, hi):
        return _dot(hn, win_ref[:, lo:hi])

    @pl.when(i == 0)
    def _():
        cext_ref[0:CONV_HALO, :] = jnp.zeros((CONV_HALO, CONV_CH), F32)
        pext_ref[0:POOL_HALO, :] = jnp.zeros((POOL_HALO, POOL_CH), F32)

    cext_ref[CONV_HALO:CONV_HALO + tm, :] = proj(*SEG_CA) * jax.nn.sigmoid(proj(*SEG_CG))
    for r in range(1, SUBLANES):
        cshift_ref[r - 1] = cext_ref[r:r + tm + CONV_HALO - SUBLANES, :]
    first_shift = CONV_HALO - (CONV_W - 1)
    n_conv_steps = tm // CONV_ROWS
    gate_slabs = (SEG_GATES[1] - SEG_GATES[0]) // MXU_DIM
    slab = 0
    for step in range(n_conv_steps):
        r0 = step * CONV_ROWS
        acc = None
        for j in range(CONV_W):
            s = first_shift + j
            r, base = s % SUBLANES, r0 + s - s % SUBLANES
            src = cext_ref[base:base + CONV_ROWS, :] if r == 0 else cshift_ref[r - 1, base:base + CONV_ROWS, :]
            term = src * cw_ref[j:j + 1, :]
            acc = term if acc is None else acc + term
        conv_ref[r0:r0 + CONV_ROWS, :] = acc
        slab_end = (step + 1) * gate_slabs // n_conv_steps
        while slab < slab_end:
            lo = slab * MXU_DIM
            gates_ref[:, lo:lo + MXU_DIM] = jax.nn.sigmoid(
                proj(SEG_GATES[0] + lo, SEG_GATES[0] + lo + MXU_DIM))
            slab += 1
    cext_ref[0:CONV_HALO, :] = cext_ref[tm:tm + CONV_HALO, :]

    cos = cos_ref[0]
    sin = sin_ref[0]

    cq = _rms(proj(*SEG_Q), qn_ref[...]).astype(BF16)
    qt = _dot(wuqt_ref[...], cq, _NT)
    for h in range(N_HEADS):
        base = h * HEAD_PAD
        r1, r2 = _rope_rows(qt[base + NOPE:base + NOPE + ROPE_HALF], qt[base + NOPE + ROPE_HALF:base + QK_DIM],
                            cos, sin)
        blk = jnp.concatenate([qt[base:base + NOPE], r1, r2, qt[base + QK_DIM:base + HEAD_PAD]], axis=0)
        qt_ref[0, base:base + HEAD_PAD, :] = (blk * Q_SCALE).astype(BF16)

    ckv = _rms(proj(*SEG_KV), kvn_ref[...]).astype(BF16)
    kpad = _dot(ckv, wuk_ref[...])
    row_id = lax.broadcasted_iota(jnp.int32, (N_HEADS * V_ROWS, 1), 0)
    ones_row = sum((row_id == h * V_ROWS + V_DIM).astype(F32) for h in range(N_HEADS))
    vt_ref[0] = (_dot(wuvt_ref[...], ckv, _NT) + ones_row).astype(BF16)
    krt = proj(*SEG_KR).T
    r1, r2 = _rope_rows(krt[0:ROPE_HALF], krt[ROPE_HALF:ROPE], cos, sin)
    kr = jnp.concatenate([jnp.zeros((NOPE, tm), F32), r1, r2, jnp.zeros((HEAD_PAD - QK_DIM, tm), F32)], axis=0).T
    for h in range(N_HEADS):
        base = h * HEAD_PAD
        k_ref[0, :, base:base + HEAD_PAD] = (kpad[:, base:base + HEAD_PAD] + kr).astype(BF16)

    hc = conv_ref[...] + cb_ref[...]
    xc = hc - jnp.mean(hc, axis=-1, keepdims=True)
    ln = xc * lax.rsqrt(jnp.mean(xc * xc, axis=-1, keepdims=True) + EPS) * lng_ref[...] + lnb_ref[...]
    yconv = _dot((ln * jax.nn.sigmoid(ln)).astype(BF16), wco_ref[...])

    u = proj(*SEG_POOL)
    pext_ref[POOL_HALO:POOL_HALO + tm, :] = u
    t1 = i * tm + 1 + lax.broadcasted_iota(jnp.int32, (tm, 1), 0)
    diffs = []
    for gi, w in enumerate(POOL_WINDOWS):
        lanes = slice(gi * POOL_GD, (gi + 1) * POOL_GD)
        tot = u[:, lanes]
        for d in range(1, w):
            tot = tot + pext_ref[POOL_HALO - d:POOL_HALO - d + tm, lanes]
        cnt = jnp.minimum(t1, w).astype(F32)
        diffs.append(tot / cnt - u[:, lanes])
    pext_ref[0:POOL_HALO, :] = pext_ref[tm:tm + POOL_HALO, :]
    dmix = jnp.concatenate(diffs, axis=1).astype(BF16)
    mixed = _dot(dmix, pbd_ref[...]) * ps_ref[...]
    ypool = _dot(mixed.astype(BF16), wpo_ref[...])

    part = gates_ref[:, D_MODEL:2 * D_MODEL] * yconv + gates_ref[:, 2 * D_MODEL:3 * D_MODEL] * ypool
    part_ref[0] = part.astype(BF16)
    g0_ref[0] = gates_ref[:, 0:D_MODEL].astype(BF16)


def _const_spec(shape):
    return pl.BlockSpec(shape, lambda b, i: (0,) * len(shape), pipeline_mode=pl.Buffered(1))


def _mixer_in(x, cos_t, sin_t, p, *, tm):
    b, s, d = x.shape
    nt = s // tm
    row_tile = lambda w: pl.BlockSpec((1, tm, w), lambda bi, i: (bi, i, 0))
    col_tile = lambda r: pl.BlockSpec((1, r, tm), lambda bi, i: (bi, 0, i))
    consts = [p["npre"], p["w_in"], p["q_norm"], p["w_uq_t"], p["kv_norm"], p["w_uk"], p["w_uv_t"],
              p["conv_w"], p["conv_b"], p["ln_g"], p["ln_b"], p["w_conv_o"], p["pool_bd"], p["pool_scale"],
              p["w_pool_o"]]
    return pl.pallas_call(
        functools.partial(_mixer_in_kernel, tm=tm),
        grid=(b, nt),
        in_specs=[row_tile(d), col_tile(ROPE_HALF), col_tile(ROPE_HALF)] + [_const_spec(c.shape) for c in consts],
        out_specs=[col_tile(N_HEADS * HEAD_PAD), row_tile(N_HEADS * HEAD_PAD), col_tile(N_HEADS * V_ROWS),
                   row_tile(d), row_tile(d)],
        out_shape=[
            jax.ShapeDtypeStruct((b, N_HEADS * HEAD_PAD, s), BF16),
            jax.ShapeDtypeStruct((b, s, N_HEADS * HEAD_PAD), BF16),
            jax.ShapeDtypeStruct((b, N_HEADS * V_ROWS, s), BF16),
            jax.ShapeDtypeStruct((b, s, d), BF16),
            jax.ShapeDtypeStruct((b, s, d), BF16),
        ],
        scratch_shapes=[
            pltpu.VMEM((tm + CONV_HALO, CONV_CH), F32),
            pltpu.VMEM((SUBLANES - 1, tm + CONV_HALO - SUBLANES, CONV_CH), F32),
            pltpu.VMEM((tm, CONV_CH), F32),
            pltpu.VMEM((tm + POOL_HALO, POOL_CH), F32),
            pltpu.VMEM((tm, N_BRANCHES * D_MODEL), F32),
        ],
        compiler_params=pltpu.CompilerParams(
            dimension_semantics=("arbitrary", "arbitrary"), vmem_limit_bytes=VMEM_LIMIT_BYTES),
        name="mixer_in",
    )(x, cos_t, sin_t, *consts)


def _attn_kernel(qt_ref, k_ref, vt_ref, ot_ref, s0_ref, s1_ref, m_ref, acc_ref, *, tq, tk):
    i = pl.program_id(1)
    m_ref[...] = jnp.full(m_ref.shape, -jnp.inf, F32)
    acc_ref[...] = jnp.zeros(acc_ref.shape, F32)

    def head(h):
        return slice(h * HEAD_PAD, (h + 1) * HEAD_PAD)

    def scores_into(s_ref, c, q_lo):
        k0 = pl.multiple_of(c * tk, tk)
        for h in range(N_HEADS):
            s_ref[h, :, 0:tq - q_lo] = _dot(k_ref[0, pl.ds(k0, tk), head(h)], qt_ref[0, head(h), q_lo:tq])

    def update(s_ref, c, q_lo, masked):
        k0 = pl.multiple_of(c * tk, tk)
        width = tq - q_lo
        probs, alphas = [], []
        for h in range(N_HEADS):
            s = s_ref[h, :, 0:width]
            if masked:
                key = lax.broadcasted_iota(jnp.int32, (tk, 1), 0)
                qry = lax.broadcasted_iota(jnp.int32, (1, width), 1)
                s = jnp.where(key <= qry, s, -jnp.inf)
            m = m_ref[h, :, q_lo:tq]
            m_new = jnp.maximum(m, jnp.max(s, axis=0, keepdims=True))
            alphas.append(jnp.exp2(m - m_new))
            probs.append(jnp.exp2(s - m_new).astype(BF16))
            m_ref[h, :, q_lo:tq] = m_new
        for h in range(N_HEADS):
            vc = vt_ref[0, h * V_ROWS:(h + 1) * V_ROWS, pl.ds(k0, tk)]
            acc_ref[h, :, q_lo:tq] = alphas[h] * acc_ref[h, :, q_lo:tq] + _dot(vc, probs[h])

    scores_into(s0_ref, 0, 0)

    def pair(p, carry):
        scores_into(s1_ref, 2 * p + 1, 0)
        update(s0_ref, 2 * p, 0, False)
        scores_into(s0_ref, 2 * p + 2, 0)
        update(s1_ref, 2 * p + 1, 0, False)
        return carry

    lax.fori_loop(0, i, pair, 0)
    scores_into(s1_ref, 2 * i + 1, tk)
    update(s0_ref, 2 * i, 0, True)
    update(s1_ref, 2 * i + 1, tk, True)
    for h in range(N_HEADS):
        ot_ref[0, h * V_DIM:(h + 1) * V_DIM, :] = (acc_ref[h, 0:V_DIM, :] / acc_ref[h, V_DIM:V_DIM + 1, :]).astype(BF16)


def _attention(q_t, k, v_t, *, tq, tk):
    b, _, s = q_t.shape
    assert tq == 2 * tk and s % tq == 0
    return pl.pallas_call(
        functools.partial(_attn_kernel, tq=tq, tk=tk),
        grid=(b, s // tq),
        in_specs=[
            pl.BlockSpec((1, N_HEADS * HEAD_PAD, tq), lambda bi, i: (bi, 0, i)),
            pl.BlockSpec((1, s, N_HEADS * HEAD_PAD), lambda bi, i: (bi, 0, 0), pipeline_mode=pl.Buffered(1)),
            pl.BlockSpec((1, N_HEADS * V_ROWS, s), lambda bi, i: (bi, 0, 0), pipeline_mode=pl.Buffered(1)),
        ],
        out_specs=pl.BlockSpec((1, N_HEADS * V_DIM, tq), lambda bi, i: (bi, 0, i)),
        out_shape=jax.ShapeDtypeStruct((b, N_HEADS * V_DIM, s), BF16),
        scratch_shapes=[
            pltpu.VMEM((N_HEADS, tk, tq), F32),
            pltpu.VMEM((N_HEADS, tk, tq), F32),
            pltpu.VMEM((N_HEADS, 1, tq), F32),
            pltpu.VMEM((N_HEADS, V_ROWS, tq), F32),
        ],
        compiler_params=pltpu.CompilerParams(
            dimension_semantics=("arbitrary", "arbitrary"), vmem_limit_bytes=VMEM_LIMIT_BYTES),
        name="attention",
    )(q_t, k, v_t)


def _merge_ffn_kernel(h_ref, ot_ref, g0_ref, part_ref, wao_ref, wmo_ref, npost_ref, fpre_ref, wg_ref, wu_ref,
                      wd_ref, fpost_ref, out_ref):
    yattn = _dot(ot_ref[0], wao_ref[...], _TN)
    merged = g0_ref[0].astype(F32) * yattn + part_ref[0].astype(F32)
    h1 = h_ref[0] + _rms(_dot(merged.astype(BF16), wmo_ref[...]), npost_ref[...])
    hn = _rms(h1, fpre_ref[...]).astype(BF16)
    gate = _dot(hn, wg_ref[...])
    act = (gate * jax.nn.sigmoid(gate) * _dot(hn, wu_ref[...])).astype(BF16)
    out_ref[0] = h1 + _rms(_dot(act, wd_ref[...]), fpost_ref[...])


def _merge_ffn(h, o_t, g0, part, p, *, tm):
    b, s, d = h.shape
    row_tile = lambda w: pl.BlockSpec((1, tm, w), lambda bi, i: (bi, i, 0))
    consts = [p["w_attn_o"], p["w_mix_o"], p["npost"], p["fpre"], p["w_gate"], p["w_up"], p["w_down"], p["fpost"]]
    return pl.pallas_call(
        _merge_ffn_kernel,
        grid=(b, s // tm),
        in_specs=[row_tile(d), pl.BlockSpec((1, N_HEADS * V_DIM, tm), lambda bi, i: (bi, 0, i)), row_tile(d),
                  row_tile(d)] + [_const_spec(c.shape) for c in consts],
        out_specs=row_tile(d),
        out_shape=jax.ShapeDtypeStruct((b, s, d), F32),
        compiler_params=pltpu.CompilerParams(
            dimension_semantics=("arbitrary", "arbitrary"), vmem_limit_bytes=VMEM_LIMIT_BYTES),
        name="merge_ffn",
    )(h, o_t, g0, part, *consts)


def _layer_params(l, mix_norm_pre, w_in, q_norm, w_uq, kv_norm, w_uk, w_uv, w_attn_o, conv_w, conv_b, conv_ln_g,
                  conv_ln_b, w_conv_o, pool_w, pool_scale, w_pool_o, w_mix_o, mix_norm_post, ffn_norm_pre, w_gate,
                  w_up, w_down, ffn_norm_post):
    row = lambda v: v[l].reshape(1, -1)
    wi = w_in[l]
    kr_hi = SEG_KR[0] + ROPE
    w_in_p = jnp.concatenate(
        [wi[:, :kr_hi], jnp.zeros((D_MODEL, SEG_KR[1] - kr_hi), wi.dtype), wi[:, kr_hi:]], axis=1)
    assert w_in_p.shape[1] == D_IN_PAD

    def pad_heads(w, dh, dh_pad):
        w = jnp.pad(w.reshape(w.shape[0], N_HEADS, dh), ((0, 0), (0, 0), (0, dh_pad - dh)))
        return w.reshape(w.shape[0], N_HEADS * dh_pad)

    pool_bd = jnp.zeros((POOL_CH, POOL_CH), F32)
    for gi in range(len(POOL_WINDOWS)):
        sl = slice(gi * POOL_GD, (gi + 1) * POOL_GD)
        pool_bd = pool_bd.at[sl, sl].set(pool_w[l, gi])
    return dict(
        npre=row(mix_norm_pre), w_in=w_in_p.astype(BF16), q_norm=row(q_norm),
        w_uq_t=pad_heads(w_uq[l], QK_DIM, HEAD_PAD).T.astype(BF16), kv_norm=row(kv_norm),
        w_uk=pad_heads(w_uk[l], NOPE, HEAD_PAD).astype(BF16),
        w_uv_t=pad_heads(w_uv[l], V_DIM, V_ROWS).T.astype(BF16),
        conv_w=conv_w[l], conv_b=row(conv_b), ln_g=row(conv_ln_g), ln_b=row(conv_ln_b),
        w_conv_o=w_conv_o[l].astype(BF16), pool_bd=pool_bd.astype(BF16), pool_scale=row(pool_scale),
        w_pool_o=w_pool_o[l].astype(BF16),
        w_attn_o=w_attn_o[l].astype(BF16), w_mix_o=w_mix_o[l].astype(BF16), npost=row(mix_norm_post),
        fpre=row(ffn_norm_pre), w_gate=w_gate[l].astype(BF16), w_up=w_up[l].astype(BF16),
        w_down=w_down[l].astype(BF16), fpost=row(ffn_norm_post),
    )


def kernel(x, positions, mix_norm_pre, w_in, q_norm, w_uq, kv_norm, w_uk, w_uv, w_attn_o, conv_w, conv_b, conv_ln_g, conv_ln_b, w_conv_o, pool_w, pool_scale, w_pool_o, w_mix_o, mix_norm_post, ffn_norm_pre, w_gate, w_up, w_down, ffn_norm_post):
    weights = (mix_norm_pre, w_in, q_norm, w_uq, kv_norm, w_uk, w_uv, w_attn_o, conv_w, conv_b, conv_ln_g,
               conv_ln_b, w_conv_o, pool_w, pool_scale, w_pool_o, w_mix_o, mix_norm_post, ffn_norm_pre, w_gate,
               w_up, w_down, ffn_norm_post)
    s = x.shape[1]
    tm = min(256, s)
    cos_t, sin_t = _rope_tables(positions)
    h = x
    for l in range(mix_norm_pre.shape[0]):
        p = _layer_params(l, *weights)
        q_t, k, v_t, g0, part = _mixer_in(h, cos_t, sin_t, p, tm=tm)
        o_t = _attention(q_t, k, v_t, tq=ATTN_TQ, tk=ATTN_TK)
        h = _merge_ffn(h, o_t, g0, part, p, tm=tm)
    return h
```

```python
import functools
import math

import jax
import jax.numpy as jnp
from jax import lax
from jax.experimental import pallas as pl
from jax.experimental.pallas import tpu as pltpu

D_MODEL = 1024
N_HEADS = 8
NOPE = 64
ROPE = 32
ROPE_HALF = ROPE // 2
V_DIM = 64
Q_RANK = 384
KV_RANK = 256
ROPE_THETA = 10000.0
CONV_CH = 512
CONV_W = 31
POOL_WINDOWS = (2, 4, 8, 16)
POOL_CH = 512
POOL_GD = POOL_CH // len(POOL_WINDOWS)
N_BRANCHES = 3
D_FF = 2816
EPS = 1e-6

LANES = 128
SUBLANES = 8
BF16_ROWS = 16
MXU_DIM = 256
VMEM_LIMIT_BYTES = 56 * 1024 * 1024

HEAD_PAD = LANES
QK_DIM = NOPE + ROPE
V_ROWS = V_DIM + BF16_ROWS
Q_SCALE = (1.0 / math.sqrt(QK_DIM)) * math.log2(math.e)

SEG_Q = (0, 384)
SEG_KV = (384, 640)
SEG_KR = (640, 768)
SEG_CA = (768, 1280)
SEG_CG = (1280, 1792)
SEG_POOL = (1792, 2304)
SEG_GATES = (2304, 5376)
D_IN_PAD = SEG_GATES[1]

CONV_HALO = 32
POOL_HALO = 16
CONV_ROWS = 32

ROW_TILE = 512
MERGE_ROWS = 256
ATTN_TQ = 512

F32 = jnp.float32
BF16 = jnp.bfloat16

_NN = (((1,), (0,)), ((), ()))
_NT = (((1,), (1,)), ((), ()))
_TN = (((0,), (0,)), ((), ()))


def _rms(x, g):
    return x * lax.rsqrt(jnp.mean(x * x, axis=-1, keepdims=True) + EPS) * g


def _dot(a, b, dims=_NN):
    return lax.dot_general(a, b, dims, preferred_element_type=F32)


def _rope_rows(x1, x2, cos, sin):
    return x1 * cos - x2 * sin, x2 * cos + x1 * sin


def _rope_table_kernel(pos_ref, invf_ref, cos_ref, sin_ref):
    ang = invf_ref[...] * pos_ref[0].astype(F32)
    cos_ref[0] = jnp.cos(ang)
    sin_ref[0] = jnp.sin(ang)


def _rope_tables(positions):
    b, s = positions.shape
    inv_freq = ROPE_THETA ** (-jnp.arange(0, ROPE, 2, dtype=F32) / ROPE)
    out = jax.ShapeDtypeStruct((b, ROPE_HALF, s), F32)
    return pl.pallas_call(
        _rope_table_kernel,
        grid=(b,),
        in_specs=[
            pl.BlockSpec((1, 1, s), lambda i: (i, 0, 0)),
            pl.BlockSpec((ROPE_HALF, 1), lambda i: (0, 0)),
        ],
        out_specs=[
            pl.BlockSpec((1, ROPE_HALF, s), lambda i: (i, 0, 0)),
            pl.BlockSpec((1, ROPE_HALF, s), lambda i: (i, 0, 0)),
        ],
        out_shape=[out, out],
        name="rope_tables",
    )(positions.reshape(b, 1, s), inv_freq.reshape(ROPE_HALF, 1))


def _mixer_in_kernel(x_ref, cos_ref, sin_ref, npre_ref, win_ref, qn_ref, wuqt_ref, kvn_ref, wuk_ref, wuvt_ref,
                     cw_ref, cb_ref, lng_ref, lnb_ref, wco_ref, pbd_ref, ps_ref, wpo_ref,
                     qt_ref, k_ref, vt_ref, g0_ref, part_ref,
                     cext_ref, cshift_ref, cwb_ref, conv_ref, pext_ref, gates_ref, *, tm):
    i = pl.program_id(1)
    hn = _rms(x_ref[0], npre_ref[...]).astype(BF16)

    def proj(lo, hi):
        return _dot(hn, win_ref[:, lo:hi])

    @pl.when(i == 0)
    def _():
        cext_ref[0:CONV_HALO, :] = jnp.zeros((CONV_HALO, CONV_CH), F32)
        pext_ref[0:POOL_HALO, :] = jnp.zeros((POOL_HALO, POOL_CH), F32)
        for j in range(CONV_W):
            cwb_ref[j] = jnp.broadcast_to(cw_ref[j:j + 1, :], (SUBLANES, CONV_CH))

    cext_ref[CONV_HALO:CONV_HALO + tm, :] = proj(*SEG_CA) * jax.nn.sigmoid(proj(*SEG_CG))
    ext_rows = tm + CONV_HALO
    ext = cext_ref[...]
    for r in range(1, SUBLANES):
        cshift_ref[r - 1] = pltpu.roll(ext, ext_rows - r, 0)[0:ext_rows - SUBLANES]
    first_shift = CONV_HALO - (CONV_W - 1)
    n_conv_steps = tm // CONV_ROWS
    gate_slabs = (SEG_GATES[1] - SEG_GATES[0]) // MXU_DIM
    slab = 0
    for step in range(n_conv_steps):
        r0 = step * CONV_ROWS
        accs = [None] * (CONV_ROWS // SUBLANES)
        for j in range(CONV_W):
            s = first_shift + j
            r, base = s % SUBLANES, r0 + s - s % SUBLANES
            w_tile = cwb_ref[j]
            for g in range(len(accs)):
                rows = slice(base + g * SUBLANES, base + (g + 1) * SUBLANES)
                src = cext_ref[rows, :] if r == 0 else cshift_ref[r - 1, rows, :]
                accs[g] = src * w_tile if accs[g] is None else accs[g] + src * w_tile
        conv_ref[r0:r0 + CONV_ROWS, :] = jnp.concatenate(accs, axis=0)
        slab_end = (step + 1) * gate_slabs // n_conv_steps
        while slab < slab_end:
            lo = slab * MXU_DIM
            gates_ref[:, lo:lo + MXU_DIM] = jax.nn.sigmoid(
                proj(SEG_GATES[0] + lo, SEG_GATES[0] + lo + MXU_DIM))
            slab += 1
    cext_ref[0:CONV_HALO, :] = cext_ref[tm:tm + CONV_HALO, :]

    cos = cos_ref[0]
    sin = sin_ref[0]

    cq = _rms(proj(*SEG_Q), qn_ref[...]).astype(BF16)
    qt = _dot(wuqt_ref[...], cq, _NT)
    for h in range(N_HEADS):
        base = h * HEAD_PAD
        r1, r2 = _rope_rows(qt[base + NOPE:base + NOPE + ROPE_HALF], qt[base + NOPE + ROPE_HALF:base + QK_DIM],
                            cos, sin)
        blk = jnp.concatenate([qt[base:base + NOPE], r1, r2, qt[base + QK_DIM:base + HEAD_PAD]], axis=0)
        qt_ref[0, base:base + HEAD_PAD, :] = (blk * Q_SCALE).astype(BF16)

    ckv = _rms(proj(*SEG_KV), kvn_ref[...]).astype(BF16)
    kpad = _dot(ckv, wuk_ref[...])
    row_id = lax.broadcasted_iota(jnp.int32, (N_HEADS * V_ROWS, 1), 0)
    ones_row = sum((row_id == h * V_ROWS + V_DIM).astype(F32) for h in range(N_HEADS))
    vt_ref[0] = (_dot(wuvt_ref[...], ckv, _NT) + ones_row).astype(BF16)
    krt = proj(*SEG_KR).T
    r1, r2 = _rope_rows(krt[0:ROPE_HALF], krt[ROPE_HALF:ROPE], cos, sin)
    kr = jnp.concatenate([jnp.zeros((NOPE, tm), F32), r1, r2, jnp.zeros((HEAD_PAD - QK_DIM, tm), F32)], axis=0).T
    for h in range(N_HEADS):
        base = h * HEAD_PAD
        k_ref[0, :, base:base + HEAD_PAD] = (kpad[:, base:base + HEAD_PAD] + kr).astype(BF16)

    hc = conv_ref[...] + cb_ref[...]
    xc = hc - jnp.mean(hc, axis=-1, keepdims=True)
    ln = xc * lax.rsqrt(jnp.mean(xc * xc, axis=-1, keepdims=True) + EPS) * lng_ref[...] + lnb_ref[...]
    yconv = _dot((ln * jax.nn.sigmoid(ln)).astype(BF16), wco_ref[...])

    u = proj(*SEG_POOL)
    pext_ref[POOL_HALO:POOL_HALO + tm, :] = u
    t1 = i * tm + 1 + lax.broadcasted_iota(jnp.int32, (tm, 1), 0)
    pool_rows = tm + POOL_HALO
    run = pext_ref[...]
    diffs = []
    for gi, w in enumerate(POOL_WINDOWS):
        run = run + pltpu.roll(run, w // 2, 0)
        tot = run[POOL_HALO:pool_rows, 0:POOL_GD]
        cnt = jnp.minimum(t1, w).astype(F32)
        diffs.append(tot / cnt - u[:, gi * POOL_GD:(gi + 1) * POOL_GD])
        run = run[:, POOL_GD:]
    pext_ref[0:POOL_HALO, :] = pext_ref[tm:tm + POOL_HALO, :]
    dmix = jnp.concatenate(diffs, axis=1).astype(BF16)
    mixed = _dot(dmix, pbd_ref[...]) * ps_ref[...]
    ypool = _dot(mixed.astype(BF16), wpo_ref[...])

    part = gates_ref[:, D_MODEL:2 * D_MODEL] * yconv + gates_ref[:, 2 * D_MODEL:3 * D_MODEL] * ypool
    part_ref[0] = part.astype(BF16)
    g0_ref[0] = gates_ref[:, 0:D_MODEL].astype(BF16)


def _layer_spec(stacked, l):
    zeros = (0,) * (stacked.ndim - 1)
    return pl.BlockSpec((None,) + stacked.shape[1:], lambda b, i: (l,) + zeros, pipeline_mode=pl.Buffered(1))


def _mixer_in(x, cos_t, sin_t, p, l, *, tm):
    b, s, d = x.shape
    nt = s // tm
    row_tile = lambda w: pl.BlockSpec((1, tm, w), lambda bi, i: (bi, i, 0))
    col_tile = lambda r: pl.BlockSpec((1, r, tm), lambda bi, i: (bi, 0, i))
    consts = [p["npre"], p["w_in"], p["q_norm"], p["w_uq_t"], p["kv_norm"], p["w_uk"], p["w_uv_t"],
              p["conv_w"], p["conv_b"], p["ln_g"], p["ln_b"], p["w_conv_o"], p["pool_bd"], p["pool_scale"],
              p["w_pool_o"]]
    return pl.pallas_call(
        functools.partial(_mixer_in_kernel, tm=tm),
        grid=(b, nt),
        in_specs=[row_tile(d), col_tile(ROPE_HALF), col_tile(ROPE_HALF)] + [_layer_spec(c, l) for c in consts],
        out_specs=[col_tile(N_HEADS * HEAD_PAD), row_tile(N_HEADS * HEAD_PAD), col_tile(N_HEADS * V_ROWS),
                   row_tile(d), row_tile(d)],
        out_shape=[
            jax.ShapeDtypeStruct((b, N_HEADS * HEAD_PAD, s), BF16),
            jax.ShapeDtypeStruct((b, s, N_HEADS * HEAD_PAD), BF16),
            jax.ShapeDtypeStruct((b, N_HEADS * V_ROWS, s), BF16),
            jax.ShapeDtypeStruct((b, s, d), BF16),
            jax.ShapeDtypeStruct((b, s, d), BF16),
        ],
        scratch_shapes=[
            pltpu.VMEM((tm + CONV_HALO, CONV_CH), F32),
            pltpu.VMEM((SUBLANES - 1, tm + CONV_HALO - SUBLANES, CONV_CH), F32),
            pltpu.VMEM((CONV_W, SUBLANES, CONV_CH), F32),
            pltpu.VMEM((tm, CONV_CH), F32),
            pltpu.VMEM((tm + POOL_HALO, POOL_CH), F32),
            pltpu.VMEM((tm, N_BRANCHES * D_MODEL), F32),
        ],
        compiler_params=pltpu.CompilerParams(
            dimension_semantics=("arbitrary", "arbitrary"), vmem_limit_bytes=VMEM_LIMIT_BYTES),
        name="mixer_in",
    )(x, cos_t, sin_t, *consts)


def _attn_kernel(qt_ref, k_ref, vt_ref, ot_ref, s0_ref, s1_ref, m_ref, acc_ref, *, tq):
    i = pl.program_id(1)
    half = tq // 2
    m_ref[...] = jnp.full(m_ref.shape, -jnp.inf, F32)
    acc_ref[...] = jnp.zeros(acc_ref.shape, F32)

    def head(h):
        return slice(h * HEAD_PAD, (h + 1) * HEAD_PAD)

    def scores_into(s_ref, c):
        k0 = pl.multiple_of(c * tq, tq)
        for h in range(N_HEADS):
            s_ref[h] = _dot(k_ref[0, pl.ds(k0, tq), head(h)], qt_ref[0, head(h), :])

    def update(s_ref, c, k_lo, n_keys, q_lo, masked):
        k0 = pl.multiple_of(c * tq + k_lo, half)
        for h in range(N_HEADS):
            s = s_ref[h, k_lo:k_lo + n_keys, q_lo:tq]
            if masked:
                key = lax.broadcasted_iota(jnp.int32, (n_keys, 1), 0) + (k_lo - q_lo)
                qry = lax.broadcasted_iota(jnp.int32, (1, tq - q_lo), 1)
                s = jnp.where(key <= qry, s, -jnp.inf)
            m = m_ref[h, :, q_lo:tq]
            m_new = jnp.maximum(m, jnp.max(s, axis=0, keepdims=True))
            alpha = jnp.exp2(m - m_new)
            prob = jnp.exp2(s - m_new).astype(BF16)
            m_ref[h, :, q_lo:tq] = m_new
            vc = vt_ref[0, h * V_ROWS:(h + 1) * V_ROWS, pl.ds(k0, n_keys)]
            acc_ref[h, :, q_lo:tq] = alpha * acc_ref[h, :, q_lo:tq] + _dot(vc, prob)

    def diagonal(s_ref):
        update(s_ref, i, 0, half, 0, True)
        update(s_ref, i, half, half, half, True)

    scores_into(s0_ref, 0)

    def pair(p, carry):
        scores_into(s1_ref, 2 * p + 1)
        update(s0_ref, 2 * p, 0, tq, 0, False)
        scores_into(s0_ref, 2 * p + 2)
        update(s1_ref, 2 * p + 1, 0, tq, 0, False)
        return carry

    lax.fori_loop(0, i // 2, pair, 0)

    @pl.when(i % 2 == 0)
    def _():
        diagonal(s0_ref)

    @pl.when(i % 2 == 1)
    def _():
        scores_into(s1_ref, i)
        update(s0_ref, i - 1, 0, tq, 0, False)
        diagonal(s1_ref)

    for h in range(N_HEADS):
        ot_ref[0, h * V_DIM:(h + 1) * V_DIM, :] = (acc_ref[h, 0:V_DIM, :] / acc_ref[h, V_DIM:V_DIM + 1, :]).astype(BF16)


def _attention(q_t, k, v_t, *, tq):
    b, _, s = q_t.shape
    tk = tq
    assert s % tq == 0
    return pl.pallas_call(
        functools.partial(_attn_kernel, tq=tq),
        grid=(b, s // tq),
        in_specs=[
            pl.BlockSpec((1, N_HEADS * HEAD_PAD, tq), lambda bi, i: (bi, 0, i)),
            pl.BlockSpec((1, s, N_HEADS * HEAD_PAD), lambda bi, i: (bi, 0, 0), pipeline_mode=pl.Buffered(1)),
            pl.BlockSpec((1, N_HEADS * V_ROWS, s), lambda bi, i: (bi, 0, 0), pipeline_mode=pl.Buffered(1)),
        ],
        out_specs=pl.BlockSpec((1, N_HEADS * V_DIM, tq), lambda bi, i: (bi, 0, i)),
        out_shape=jax.ShapeDtypeStruct((b, N_HEADS * V_DIM, s), BF16),
        scratch_shapes=[
            pltpu.VMEM((N_HEADS, tk, tq), F32),
            pltpu.VMEM((N_HEADS, tk, tq), F32),
            pltpu.VMEM((N_HEADS, 1, tq), F32),
            pltpu.VMEM((N_HEADS, V_ROWS, tq), F32),
        ],
        compiler_params=pltpu.CompilerParams(
            dimension_semantics=("arbitrary", "arbitrary"), vmem_limit_bytes=VMEM_LIMIT_BYTES),
        name="attention",
    )(q_t, k, v_t)


def _merge_ffn_kernel(h_ref, ot_ref, g0_ref, part_ref, wao_ref, wmo_ref, npost_ref, fpre_ref, wg_ref, wu_ref,
                      wd_ref, fpost_ref, out_ref):
    tm = h_ref.shape[1]
    groups = [slice(r, r + MERGE_ROWS) for r in range(0, tm, MERGE_ROWS)]
    h1 = []
    for rows in groups:
        yattn = _dot(ot_ref[0, :, rows], wao_ref[...], _TN)
        merged = g0_ref[0, rows, :].astype(F32) * yattn + part_ref[0, rows, :].astype(F32)
        h1.append(h_ref[0, rows, :] + _rms(_dot(merged.astype(BF16), wmo_ref[...]), npost_ref[...]))
    act = []
    for g in range(len(groups)):
        hn = _rms(h1[g], fpre_ref[...]).astype(BF16)
        gate = _dot(hn, wg_ref[...])
        act.append((gate * jax.nn.sigmoid(gate) * _dot(hn, wu_ref[...])).astype(BF16))
    for g, rows in enumerate(groups):
        out_ref[0, rows, :] = h1[g] + _rms(_dot(act[g], wd_ref[...]), fpost_ref[...])


def _merge_ffn(h, o_t, g0, part, p, l, *, tm):
    b, s, d = h.shape
    row_tile = lambda w: pl.BlockSpec((1, tm, w), lambda bi, i: (bi, i, 0))
    consts = [p["w_attn_o"], p["w_mix_o"], p["npost"], p["fpre"], p["w_gate"], p["w_up"], p["w_down"], p["fpost"]]
    return pl.pallas_call(
        _merge_ffn_kernel,
        grid=(b, s // tm),
        in_specs=[row_tile(d), pl.BlockSpec((1, N_HEADS * V_DIM, tm), lambda bi, i: (bi, 0, i)), row_tile(d),
                  row_tile(d)] + [_layer_spec(c, l) for c in consts],
        out_specs=row_tile(d),
        out_shape=jax.ShapeDtypeStruct((b, s, d), F32),
        compiler_params=pltpu.CompilerParams(
            dimension_semantics=("arbitrary", "arbitrary"), vmem_limit_bytes=VMEM_LIMIT_BYTES),
        name="merge_ffn",
    )(h, o_t, g0, part, *consts)


def _stacked_params(mix_norm_pre, w_in, q_norm, w_uq, kv_norm, w_uk, w_uv, w_attn_o, conv_w, conv_b, conv_ln_g,
                    conv_ln_b, w_conv_o, pool_w, pool_scale, w_pool_o, w_mix_o, mix_norm_post, ffn_norm_pre, w_gate,
                    w_up, w_down, ffn_norm_post):
    depth = w_in.shape[0]
    row = lambda v: v.reshape(depth, 1, -1)
    kr_hi = SEG_KR[0] + ROPE
    w_in_p = jnp.concatenate(
        [w_in[:, :, :kr_hi].astype(BF16), jnp.zeros((depth, D_MODEL, SEG_KR[1] - kr_hi), BF16),
         w_in[:, :, kr_hi:].astype(BF16)], axis=2)
    assert w_in_p.shape[2] == D_IN_PAD

    def pad_heads(w, dh, dh_pad):
        w = jnp.pad(w.reshape(depth, w.shape[1], N_HEADS, dh), ((0, 0), (0, 0), (0, 0), (0, dh_pad - dh)))
        return w.reshape(depth, w.shape[1], N_HEADS * dh_pad)

    transpose = lambda w: jnp.swapaxes(w, 1, 2)
    pool_bd = jnp.zeros((depth, POOL_CH, POOL_CH), F32)
    for gi in range(len(POOL_WINDOWS)):
        sl = slice(gi * POOL_GD, (gi + 1) * POOL_GD)
        pool_bd = pool_bd.at[:, sl, sl].set(pool_w[:, gi])
    return dict(
        npre=row(mix_norm_pre), w_in=w_in_p, q_norm=row(q_norm),
        w_uq_t=transpose(pad_heads(w_uq, QK_DIM, HEAD_PAD)).astype(BF16), kv_norm=row(kv_norm),
        w_uk=pad_heads(w_uk, NOPE, HEAD_PAD).astype(BF16),
        w_uv_t=transpose(pad_heads(w_uv, V_DIM, V_ROWS)).astype(BF16),
        conv_w=conv_w, conv_b=row(conv_b), ln_g=row(conv_ln_g), ln_b=row(conv_ln_b),
        w_conv_o=w_conv_o.astype(BF16), pool_bd=pool_bd.astype(BF16), pool_scale=row(pool_scale),
        w_pool_o=w_pool_o.astype(BF16),
        w_attn_o=w_attn_o.astype(BF16), w_mix_o=w_mix_o.astype(BF16), npost=row(mix_norm_post),
        fpre=row(ffn_norm_pre), w_gate=w_gate.astype(BF16), w_up=w_up.astype(BF16),
        w_down=w_down.astype(BF16), fpost=row(ffn_norm_post),
    )


def kernel(x, positions, mix_norm_pre, w_in, q_norm, w_uq, kv_norm, w_uk, w_uv, w_attn_o, conv_w, conv_b, conv_ln_g, conv_ln_b, w_conv_o, pool_w, pool_scale, w_pool_o, w_mix_o, mix_norm_post, ffn_norm_pre, w_gate, w_up, w_down, ffn_norm_post):
    p = _stacked_params(mix_norm_pre, w_in, q_norm, w_uq, kv_norm, w_uk, w_uv, w_attn_o, conv_w, conv_b, conv_ln_g,
                        conv_ln_b, w_conv_o, pool_w, pool_scale, w_pool_o, w_mix_o, mix_norm_post, ffn_norm_pre,
                        w_gate, w_up, w_down, ffn_norm_post)
    s = x.shape[1]
    tm = min(ROW_TILE, s)
    cos_t, sin_t = _rope_tables(positions)
    h = x
    for l in range(w_in.shape[0]):
        q_t, k, v_t, g0, part = _mixer_in(h, cos_t, sin_t, p, l, tm=tm)
        o_t = _attention(q_t, k, v_t, tq=min(ATTN_TQ, s))
        h = _merge_ffn(h, o_t, g0, part, p, l, tm=tm)
    return h
```

```python
import functools
import math

import jax
import jax.numpy as jnp
from jax import lax
from jax.experimental import pallas as pl
from jax.experimental.pallas import tpu as pltpu

D_MODEL = 1024
N_HEADS = 8
NOPE = 64
ROPE = 32
ROPE_HALF = ROPE // 2
V_DIM = 64
Q_RANK = 384
KV_RANK = 256
ROPE_THETA = 10000.0
CONV_CH = 512
CONV_W = 31
POOL_WINDOWS = (2, 4, 8, 16)
POOL_CH = 512
POOL_GD = POOL_CH // len(POOL_WINDOWS)
N_BRANCHES = 3
D_FF = 2816
EPS = 1e-6

LANES = 128
SUBLANES = 8
BF16_ROWS = 16
MXU_DIM = 256
VMEM_LIMIT_BYTES = 56 * 1024 * 1024

HEAD_PAD = LANES
QK_DIM = NOPE + ROPE
V_ROWS = V_DIM + BF16_ROWS
Q_SCALE = (1.0 / math.sqrt(QK_DIM)) * math.log2(math.e)

LAT_COLS = Q_RANK + KV_RANK
WIDE_START = LAT_COLS + ROPE
SEG_Q = (0, Q_RANK)
SEG_KV = (Q_RANK, LAT_COLS)
SEG_CA = (0, 512)
SEG_CG = (512, 1024)
SEG_POOL = (1024, 1536)
SEG_GATES = (1536, 4608)

CONV_HALO = 32
POOL_HALO = 16
CONV_ROWS = 32

ROW_TILE = 512
MERGE_ROWS = 256
ATTN_TQ = 512

F32 = jnp.float32
BF16 = jnp.bfloat16

_NN = (((1,), (0,)), ((), ()))
_NT = (((1,), (1,)), ((), ()))
_TN = (((0,), (0,)), ((), ()))


def _rms(x, g):
    return x * lax.rsqrt(jnp.mean(x * x, axis=-1, keepdims=True) + EPS) * g


def _dot(a, b, dims=_NN):
    return lax.dot_general(a, b, dims, preferred_element_type=F32)


def _rope_rows(x1, x2, cos, sin):
    return x1 * cos - x2 * sin, x2 * cos + x1 * sin


def _rope_table_kernel(pos_ref, invf_ref, cos_ref, sin_ref):
    ang = invf_ref[...] * pos_ref[0].astype(F32)
    cos_ref[0] = jnp.cos(ang)
    sin_ref[0] = jnp.sin(ang)


def _rope_tables(positions):
    b, s = positions.shape
    inv_freq = ROPE_THETA ** (-jnp.arange(0, ROPE, 2, dtype=F32) / ROPE)
    out = jax.ShapeDtypeStruct((b, ROPE_HALF, s), F32)
    return pl.pallas_call(
        _rope_table_kernel,
        grid=(b,),
        in_specs=[
            pl.BlockSpec((1, 1, s), lambda i: (i, 0, 0)),
            pl.BlockSpec((ROPE_HALF, 1), lambda i: (0, 0)),
        ],
        out_specs=[
            pl.BlockSpec((1, ROPE_HALF, s), lambda i: (i, 0, 0)),
            pl.BlockSpec((1, ROPE_HALF, s), lambda i: (i, 0, 0)),
        ],
        out_shape=[out, out],
        name="rope_tables",
    )(positions.reshape(b, 1, s), inv_freq.reshape(ROPE_HALF, 1))


def _mixer_in_kernel(x_ref, cos_ref, sin_ref, npre_ref, wlat_ref, wkr_ref, wwide_ref, qn_ref, wuqt_ref, kvn_ref, wuk_ref, wuvt_ref,
                     cw_ref, cb_ref, lng_ref, lnb_ref, wco_ref, pbd_ref, ps_ref, wpo_ref,
                     qt_ref, k_ref, vt_ref, g0_ref, part_ref,
                     cext_ref, cshift_ref, cwb_ref, conv_ref, pext_ref, gates_ref, *, tm):
    i = pl.program_id(1)
    hn = [_rms(x_ref[0, r:r + MERGE_ROWS, :], npre_ref[...]).astype(BF16) for r in range(0, tm, MERGE_ROWS)]

    def proj(w_ref, lo, hi):
        return jnp.concatenate([_dot(g, w_ref[:, lo:hi]) for g in hn], axis=0)

    @pl.when(i == 0)
    def _():
        cext_ref[0:CONV_HALO, :] = jnp.zeros((CONV_HALO, CONV_CH), F32)
        pext_ref[0:POOL_HALO, :] = jnp.zeros((POOL_HALO, POOL_CH), F32)
        for j in range(CONV_W):
            cwb_ref[j] = jnp.broadcast_to(cw_ref[j:j + 1, :], (SUBLANES, CONV_CH))

    cext_ref[CONV_HALO:CONV_HALO + tm, :] = proj(wwide_ref, *SEG_CA) * jax.nn.sigmoid(proj(wwide_ref, *SEG_CG))
    ext_rows = tm + CONV_HALO
    ext = cext_ref[...]
    for r in range(1, SUBLANES):
        cshift_ref[r - 1] = pltpu.roll(ext, ext_rows - r, 0)[0:ext_rows - SUBLANES]
    first_shift = CONV_HALO - (CONV_W - 1)
    n_conv_steps = tm // CONV_ROWS
    gate_slabs = (SEG_GATES[1] - SEG_GATES[0]) // MXU_DIM
    slab = 0
    for step in range(n_conv_steps):
        r0 = step * CONV_ROWS
        accs = [None] * (CONV_ROWS // SUBLANES)
        for j in range(CONV_W):
            s = first_shift + j
            r, base = s % SUBLANES, r0 + s - s % SUBLANES
            w_tile = cwb_ref[j]
            for g in range(len(accs)):
                rows = slice(base + g * SUBLANES, base + (g + 1) * SUBLANES)
                src = cext_ref[rows, :] if r == 0 else cshift_ref[r - 1, rows, :]
                accs[g] = src * w_tile if accs[g] is None else accs[g] + src * w_tile
        conv_ref[r0:r0 + CONV_ROWS, :] = jnp.concatenate(accs, axis=0)
        slab_end = (step + 1) * gate_slabs // n_conv_steps
        while slab < slab_end:
            lo = slab * MXU_DIM
            gates_ref[:, lo:lo + MXU_DIM] = jax.nn.sigmoid(
                proj(wwide_ref, SEG_GATES[0] + lo, SEG_GATES[0] + lo + MXU_DIM))
            slab += 1
    cext_ref[0:CONV_HALO, :] = cext_ref[tm:tm + CONV_HALO, :]

    cos = cos_ref[0]
    sin = sin_ref[0]

    cq = _rms(proj(wlat_ref, *SEG_Q), qn_ref[...]).astype(BF16)
    qt = _dot(wuqt_ref[...], cq, _NT)
    for h in range(N_HEADS):
        base = h * HEAD_PAD
        r1, r2 = _rope_rows(qt[base + NOPE:base + NOPE + ROPE_HALF], qt[base + NOPE + ROPE_HALF:base + QK_DIM],
                            cos, sin)
        blk = jnp.concatenate([qt[base:base + NOPE], r1, r2, qt[base + QK_DIM:base + HEAD_PAD]], axis=0)
        qt_ref[0, base:base + HEAD_PAD, :] = (blk * Q_SCALE).astype(BF16)

    ckv = _rms(proj(wlat_ref, *SEG_KV), kvn_ref[...]).astype(BF16)
    kpad = _dot(ckv, wuk_ref[...])
    row_id = lax.broadcasted_iota(jnp.int32, (N_HEADS * V_ROWS, 1), 0)
    ones_row = sum((row_id == h * V_ROWS + V_DIM).astype(F32) for h in range(N_HEADS))
    vt_ref[0] = (_dot(wuvt_ref[...], ckv, _NT) + ones_row).astype(BF16)
    krt = proj(wkr_ref, 0, HEAD_PAD).T
    r1, r2 = _rope_rows(krt[0:ROPE_HALF], krt[ROPE_HALF:ROPE], cos, sin)
    kr = jnp.concatenate([jnp.zeros((NOPE, tm), F32), r1, r2, jnp.zeros((HEAD_PAD - QK_DIM, tm), F32)], axis=0).T
    for h in range(N_HEADS):
        base = h * HEAD_PAD
        k_ref[0, :, base:base + HEAD_PAD] = (kpad[:, base:base + HEAD_PAD] + kr).astype(BF16)

    hc = conv_ref[...] + cb_ref[...]
    xc = hc - jnp.mean(hc, axis=-1, keepdims=True)
    ln = xc * lax.rsqrt(jnp.mean(xc * xc, axis=-1, keepdims=True) + EPS) * lng_ref[...] + lnb_ref[...]
    yconv = _dot((ln * jax.nn.sigmoid(ln)).astype(BF16), wco_ref[...])

    u = proj(wwide_ref, *SEG_POOL)
    pext_ref[POOL_HALO:POOL_HALO + tm, :] = u
    t1 = i * tm + 1 + lax.broadcasted_iota(jnp.int32, (tm, 1), 0)
    pool_rows = tm + POOL_HALO
    run = pext_ref[...]
    diffs = []
    for gi, w in enumerate(POOL_WINDOWS):
        run = run + pltpu.roll(run, w // 2, 0)
        tot = run[POOL_HALO:pool_rows, 0:POOL_GD]
        cnt = jnp.minimum(t1, w).astype(F32)
        diffs.append(tot / cnt - u[:, gi * POOL_GD:(gi + 1) * POOL_GD])
        run = run[:, POOL_GD:]
    pext_ref[0:POOL_HALO, :] = pext_ref[tm:tm + POOL_HALO, :]
    dmix = jnp.concatenate(diffs, axis=1).astype(BF16)
    mixed = _dot(dmix, pbd_ref[...]) * ps_ref[...]
    ypool = _dot(mixed.astype(BF16), wpo_ref[...])

    part = gates_ref[:, D_MODEL:2 * D_MODEL] * yconv + gates_ref[:, 2 * D_MODEL:3 * D_MODEL] * ypool
    part_ref[0] = part.astype(BF16)
    g0_ref[0] = gates_ref[:, 0:D_MODEL].astype(BF16)


def _layer_spec(stacked, l):
    zeros = (0,) * (stacked.ndim - 1)
    return pl.BlockSpec((None,) + stacked.shape[1:], lambda b, i: (l,) + zeros, pipeline_mode=pl.Buffered(1))


def _mixer_in(x, cos_t, sin_t, p, l, *, tm):
    b, s, d = x.shape
    nt = s // tm
    row_tile = lambda w: pl.BlockSpec((1, tm, w), lambda bi, i: (bi, i, 0))
    col_tile = lambda r: pl.BlockSpec((1, r, tm), lambda bi, i: (bi, 0, i))
    consts = [p["npre"], p["w_lat"], p["w_kr"], p["w_wide"], p["q_norm"], p["w_uq_t"], p["kv_norm"], p["w_uk"], p["w_uv_t"],
              p["conv_w"], p["conv_b"], p["ln_g"], p["ln_b"], p["w_conv_o"], p["pool_bd"], p["pool_scale"],
              p["w_pool_o"]]
    return pl.pallas_call(
        functools.partial(_mixer_in_kernel, tm=tm),
        grid=(b, nt),
        in_specs=[row_tile(d), col_tile(ROPE_HALF), col_tile(ROPE_HALF)] + [_layer_spec(c, l) for c in consts],
        out_specs=[col_tile(N_HEADS * HEAD_PAD), row_tile(N_HEADS * HEAD_PAD), col_tile(N_HEADS * V_ROWS),
                   row_tile(d), row_tile(d)],
        out_shape=[
            jax.ShapeDtypeStruct((b, N_HEADS * HEAD_PAD, s), BF16),
            jax.ShapeDtypeStruct((b, s, N_HEADS * HEAD_PAD), BF16),
            jax.ShapeDtypeStruct((b, N_HEADS * V_ROWS, s), BF16),
            jax.ShapeDtypeStruct((b, s, d), BF16),
            jax.ShapeDtypeStruct((b, s, d), BF16),
        ],
        scratch_shapes=[
            pltpu.VMEM((tm + CONV_HALO, CONV_CH), F32),
            pltpu.VMEM((SUBLANES - 1, tm + CONV_HALO - SUBLANES, CONV_CH), F32),
            pltpu.VMEM((CONV_W, SUBLANES, CONV_CH), F32),
            pltpu.VMEM((tm, CONV_CH), F32),
            pltpu.VMEM((tm + POOL_HALO, POOL_CH), F32),
            pltpu.VMEM((tm, N_BRANCHES * D_MODEL), F32),
        ],
        compiler_params=pltpu.CompilerParams(
            dimension_semantics=("arbitrary", "arbitrary"), vmem_limit_bytes=VMEM_LIMIT_BYTES),
        name="mixer_in",
    )(x, cos_t, sin_t, *consts)


def _attn_kernel(qt_ref, k_ref, vt_ref, ot_ref, s0_ref, s1_ref, m_ref, acc_ref, *, tq):
    i = pl.program_id(1)
    half = tq // 2
    m_ref[...] = jnp.full(m_ref.shape, -jnp.inf, F32)
    acc_ref[...] = jnp.zeros(acc_ref.shape, F32)

    def head(h):
        return slice(h * HEAD_PAD, (h + 1) * HEAD_PAD)

    def scores_into(s_ref, c):
        k0 = pl.multiple_of(c * tq, tq)
        for h in range(N_HEADS):
            s_ref[h] = _dot(k_ref[0, pl.ds(k0, tq), head(h)], qt_ref[0, head(h), :])

    def update(s_ref, c, k_lo, n_keys, q_lo, masked):
        k0 = pl.multiple_of(c * tq + k_lo, half)
        for h in range(N_HEADS):
            s = s_ref[h, k_lo:k_lo + n_keys, q_lo:tq]
            if masked:
                key = lax.broadcasted_iota(jnp.int32, (n_keys, 1), 0) + (k_lo - q_lo)
                qry = lax.broadcasted_iota(jnp.int32, (1, tq - q_lo), 1)
                s = jnp.where(key <= qry, s, -jnp.inf)
            m = m_ref[h, :, q_lo:tq]
            m_new = jnp.maximum(m, jnp.max(s, axis=0, keepdims=True))
            alpha = jnp.exp2(m - m_new)
            prob = jnp.exp2(s - m_new).astype(BF16)
            m_ref[h, :, q_lo:tq] = m_new
            vc = vt_ref[0, h * V_ROWS:(h + 1) * V_ROWS, pl.ds(k0, n_keys)]
            acc_ref[h, :, q_lo:tq] = alpha * acc_ref[h, :, q_lo:tq] + _dot(vc, prob)

    def diagonal(s_ref):
        update(s_ref, i, 0, half, 0, True)
        update(s_ref, i, half, half, half, True)

    scores_into(s0_ref, 0)

    def pair(p, carry):
        scores_into(s1_ref, 2 * p + 1)
        update(s0_ref, 2 * p, 0, tq, 0, False)
        scores_into(s0_ref, 2 * p + 2)
        update(s1_ref, 2 * p + 1, 0, tq, 0, False)
        return carry

    lax.fori_loop(0, i // 2, pair, 0)

    @pl.when(i % 2 == 0)
    def _():
        diagonal(s0_ref)

    @pl.when(i % 2 == 1)
    def _():
        scores_into(s1_ref, i)
        update(s0_ref, i - 1, 0, tq, 0, False)
        diagonal(s1_ref)

    for h in range(N_HEADS):
        ot_ref[0, h * V_DIM:(h + 1) * V_DIM, :] = (acc_ref[h, 0:V_DIM, :] / acc_ref[h, V_DIM:V_DIM + 1, :]).astype(BF16)


def _attention(q_t, k, v_t, *, tq):
    b, _, s = q_t.shape
    tk = tq
    assert s % tq == 0
    return pl.pallas_call(
        functools.partial(_attn_kernel, tq=tq),
        grid=(b, s // tq),
        in_specs=[
            pl.BlockSpec((1, N_HEADS * HEAD_PAD, tq), lambda bi, i: (bi, 0, i)),
            pl.BlockSpec((1, s, N_HEADS * HEAD_PAD), lambda bi, i: (bi, 0, 0)),
            pl.BlockSpec((1, N_HEADS * V_ROWS, s), lambda bi, i: (bi, 0, 0)),
        ],
        out_specs=pl.BlockSpec((1, N_HEADS * V_DIM, tq), lambda bi, i: (bi, 0, i)),
        out_shape=jax.ShapeDtypeStruct((b, N_HEADS * V_DIM, s), BF16),
        scratch_shapes=[
            pltpu.VMEM((N_HEADS, tk, tq), F32),
            pltpu.VMEM((N_HEADS, tk, tq), F32),
            pltpu.VMEM((N_HEADS, 1, tq), F32),
            pltpu.VMEM((N_HEADS, V_ROWS, tq), F32),
        ],
        compiler_params=pltpu.CompilerParams(
            dimension_semantics=("arbitrary", "arbitrary"), vmem_limit_bytes=VMEM_LIMIT_BYTES),
        name="attention",
    )(q_t, k, v_t)


def _merge_ffn_kernel(h_ref, ot_ref, g0_ref, part_ref, wao_ref, wmo_ref, npost_ref, fpre_ref, wg_ref, wu_ref,
                      wd_ref, fpost_ref, out_ref):
    tm = h_ref.shape[1]
    groups = [slice(r, r + MERGE_ROWS) for r in range(0, tm, MERGE_ROWS)]
    h1 = []
    for rows in groups:
        yattn = _dot(ot_ref[0, :, rows], wao_ref[...], _TN)
        merged = g0_ref[0, rows, :].astype(F32) * yattn + part_ref[0, rows, :].astype(F32)
        h1.append(h_ref[0, rows, :] + _rms(_dot(merged.astype(BF16), wmo_ref[...]), npost_ref[...]))
    act = []
    for g in range(len(groups)):
        hn = _rms(h1[g], fpre_ref[...]).astype(BF16)
        gate = _dot(hn, wg_ref[...])
        act.append((gate * jax.nn.sigmoid(gate) * _dot(hn, wu_ref[...])).astype(BF16))
    for g, rows in enumerate(groups):
        out_ref[0, rows, :] = h1[g] + _rms(_dot(act[g], wd_ref[...]), fpost_ref[...])


def _merge_ffn(h, o_t, g0, part, p, l, *, tm):
    b, s, d = h.shape
    row_tile = lambda w: pl.BlockSpec((1, tm, w), lambda bi, i: (bi, i, 0))
    consts = [p["w_attn_o"], p["w_mix_o"], p["npost"], p["fpre"], p["w_gate"], p["w_up"], p["w_down"], p["fpost"]]
    return pl.pallas_call(
        _merge_ffn_kernel,
        grid=(b, s // tm),
        in_specs=[row_tile(d), pl.BlockSpec((1, N_HEADS * V_DIM, tm), lambda bi, i: (bi, 0, i)), row_tile(d),
                  row_tile(d)] + [_layer_spec(c, l) for c in consts],
        out_specs=row_tile(d),
        out_shape=jax.ShapeDtypeStruct((b, s, d), F32),
        compiler_params=pltpu.CompilerParams(
            dimension_semantics=("arbitrary", "arbitrary"), vmem_limit_bytes=VMEM_LIMIT_BYTES),
        name="merge_ffn",
    )(h, o_t, g0, part, *consts)


def _stacked_params(mix_norm_pre, w_in, q_norm, w_uq, kv_norm, w_uk, w_uv, w_attn_o, conv_w, conv_b, conv_ln_g,
                    conv_ln_b, w_conv_o, pool_w, pool_scale, w_pool_o, w_mix_o, mix_norm_post, ffn_norm_pre, w_gate,
                    w_up, w_down, ffn_norm_post):
    depth = w_in.shape[0]
    row = lambda v: v.reshape(depth, 1, -1)
    w_kr = jnp.pad(w_in[:, :, LAT_COLS:WIDE_START], ((0, 0), (0, 0), (0, HEAD_PAD - ROPE))).astype(BF16)
    assert w_in.shape[2] - WIDE_START == SEG_GATES[1]

    def pad_heads(w, dh, dh_pad):
        w = jnp.pad(w.reshape(depth, w.shape[1], N_HEADS, dh), ((0, 0), (0, 0), (0, 0), (0, dh_pad - dh)))
        return w.reshape(depth, w.shape[1], N_HEADS * dh_pad)

    transpose = lambda w: jnp.swapaxes(w, 1, 2)
    pool_bd = jnp.zeros((depth, POOL_CH, POOL_CH), F32)
    for gi in range(len(POOL_WINDOWS)):
        sl = slice(gi * POOL_GD, (gi + 1) * POOL_GD)
        pool_bd = pool_bd.at[:, sl, sl].set(pool_w[:, gi])
    return dict(
        npre=row(mix_norm_pre), w_lat=w_in[:, :, :LAT_COLS].astype(BF16), w_kr=w_kr,
        w_wide=w_in[:, :, WIDE_START:].astype(BF16), q_norm=row(q_norm),
        w_uq_t=transpose(pad_heads(w_uq, QK_DIM, HEAD_PAD)).astype(BF16), kv_norm=row(kv_norm),
        w_uk=pad_heads(w_uk, NOPE, HEAD_PAD).astype(BF16),
        w_uv_t=transpose(pad_heads(w_uv, V_DIM, V_ROWS)).astype(BF16),
        conv_w=conv_w, conv_b=row(conv_b), ln_g=row(conv_ln_g), ln_b=row(conv_ln_b),
        w_conv_o=w_conv_o.astype(BF16), pool_bd=pool_bd.astype(BF16), pool_scale=row(pool_scale),
        w_pool_o=w_pool_o.astype(BF16),
        w_attn_o=w_attn_o.astype(BF16), w_mix_o=w_mix_o.astype(BF16), npost=row(mix_norm_post),
        fpre=row(ffn_norm_pre), w_gate=w_gate.astype(BF16), w_up=w_up.astype(BF16),
        w_down=w_down.astype(BF16), fpost=row(ffn_norm_post),
    )


def kernel(x, positions, mix_norm_pre, w_in, q_norm, w_uq, kv_norm, w_uk, w_uv, w_attn_o, conv_w, conv_b, conv_ln_g, conv_ln_b, w_conv_o, pool_w, pool_scale, w_pool_o, w_mix_o, mix_norm_post, ffn_norm_pre, w_gate, w_up, w_down, ffn_norm_post):
    p = _stacked_params(mix_norm_pre, w_in, q_norm, w_uq, kv_norm, w_uk, w_uv, w_attn_o, conv_w, conv_b, conv_ln_g,
                        conv_ln_b, w_conv_o, pool_w, pool_scale, w_pool_o, w_mix_o, mix_norm_post, ffn_norm_pre,
                        w_gate, w_up, w_down, ffn_norm_post)
    s = x.shape[1]
    tm = min(ROW_TILE, s)
    cos_t, sin_t = _rope_tables(positions)
    h = x
    for l in range(w_in.shape[0]):
        q_t, k, v_t, g0, part = _mixer_in(h, cos_t, sin_t, p, l, tm=tm)
        o_t = _attention(q_t, k, v_t, tq=min(ATTN_TQ, s))
        h = _merge_ffn(h, o_t, g0, part, p, l, tm=tm)
    return h
```

```python
import functools
import math

import jax
import jax.numpy as jnp
from jax import lax
from jax.experimental import pallas as pl
from jax.experimental.pallas import tpu as pltpu

D_MODEL = 1024
N_HEADS = 8
NOPE = 64
ROPE = 32
ROPE_HALF = ROPE // 2
V_DIM = 64
Q_RANK = 384
KV_RANK = 256
ROPE_THETA = 10000.0
CONV_CH = 512
CONV_W = 31
POOL_WINDOWS = (2, 4, 8, 16)
POOL_CH = 512
POOL_GD = POOL_CH // len(POOL_WINDOWS)
N_BRANCHES = 3
D_FF = 2816
EPS = 1e-6

LANES = 128
SUBLANES = 8
BF16_ROWS = 16
MXU_DIM = 256
VMEM_LIMIT_BYTES = 56 * 1024 * 1024

HEAD_PAD = LANES
QK_DIM = NOPE + ROPE
V_ROWS = V_DIM + BF16_ROWS
Q_SCALE = (1.0 / math.sqrt(QK_DIM)) * math.log2(math.e)

LAT_COLS = Q_RANK + KV_RANK
WIDE_START = LAT_COLS + ROPE
SEG_Q = (0, Q_RANK)
SEG_KV = (Q_RANK, LAT_COLS)
SEG_CA = (0, 512)
SEG_CG = (512, 1024)
SEG_POOL = (1024, 1536)
SEG_GATES = (1536, 4608)

CONV_HALO = 32
POOL_HALO = 16
CONV_ROWS = 32

ROW_TILE = 512
MERGE_ROWS = 256
ATTN_TQ = 512

F32 = jnp.float32
BF16 = jnp.bfloat16

_NN = (((1,), (0,)), ((), ()))
_NT = (((1,), (1,)), ((), ()))
_TN = (((0,), (0,)), ((), ()))


def _rms(x, g):
    return x * lax.rsqrt(jnp.mean(x * x, axis=-1, keepdims=True) + EPS) * g


def _dot(a, b, dims=_NN):
    return lax.dot_general(a, b, dims, preferred_element_type=F32)


def _rope_rows(x1, x2, cos, sin):
    return x1 * cos - x2 * sin, x2 * cos + x1 * sin


def _rope_table_kernel(pos_ref, invf_ref, cos_ref, sin_ref):
    ang = invf_ref[...] * pos_ref[0].astype(F32)
    cos_ref[0] = jnp.cos(ang)
    sin_ref[0] = jnp.sin(ang)


def _rope_tables(positions):
    b, s = positions.shape
    inv_freq = ROPE_THETA ** (-jnp.arange(0, ROPE, 2, dtype=F32) / ROPE)
    out = jax.ShapeDtypeStruct((b, ROPE_HALF, s), F32)
    return pl.pallas_call(
        _rope_table_kernel,
        grid=(b,),
        in_specs=[
            pl.BlockSpec((1, 1, s), lambda i: (i, 0, 0)),
            pl.BlockSpec((ROPE_HALF, 1), lambda i: (0, 0)),
        ],
        out_specs=[
            pl.BlockSpec((1, ROPE_HALF, s), lambda i: (i, 0, 0)),
            pl.BlockSpec((1, ROPE_HALF, s), lambda i: (i, 0, 0)),
        ],
        out_shape=[out, out],
        name="rope_tables",
    )(positions.reshape(b, 1, s), inv_freq.reshape(ROPE_HALF, 1))


def _mixer_in_kernel(x_ref, cos_ref, sin_ref, npre_ref, wlat_ref, wkr_ref, wwide_ref, qn_ref, wuqt_ref, kvn_ref, wuk_ref, wuvt_ref,
                     cw_ref, cb_ref, lng_ref, lnb_ref, wco_ref, pbd_ref, ps_ref, wpo_ref,
                     qt_ref, k_ref, vt_ref, g0_ref, part_ref,
                     cext_ref, cshift_ref, cwb_ref, conv_ref, pext_ref, gates_ref, *, tm):
    i = pl.program_id(1)
    hn = [_rms(x_ref[0, r:r + MERGE_ROWS, :], npre_ref[...]).astype(BF16) for r in range(0, tm, MERGE_ROWS)]

    def proj(w_ref, lo, hi):
        return jnp.concatenate([_dot(g, w_ref[:, lo:hi]) for g in hn], axis=0)

    @pl.when(i == 0)
    def _():
        cext_ref[0:CONV_HALO, :] = jnp.zeros((CONV_HALO, CONV_CH), F32)
        pext_ref[0:POOL_HALO, :] = jnp.zeros((POOL_HALO, POOL_CH), F32)
        for j in range(CONV_W):
            cwb_ref[j] = jnp.broadcast_to(cw_ref[j:j + 1, :], (SUBLANES, CONV_CH))

    cext_ref[CONV_HALO:CONV_HALO + tm, :] = proj(wwide_ref, *SEG_CA) * jax.nn.sigmoid(proj(wwide_ref, *SEG_CG))
    ext_rows = tm + CONV_HALO
    ext = cext_ref[...]
    for r in range(1, SUBLANES):
        cshift_ref[r - 1] = pltpu.roll(ext, ext_rows - r, 0)[0:ext_rows - SUBLANES]
    first_shift = CONV_HALO - (CONV_W - 1)
    n_conv_steps = tm // CONV_ROWS
    gate_slabs = (SEG_GATES[1] - SEG_GATES[0]) // MXU_DIM
    slab = 0
    for step in range(n_conv_steps):
        r0 = step * CONV_ROWS
        accs = [None] * (CONV_ROWS // SUBLANES)
        for j in range(CONV_W):
            s = first_shift + j
            r, base = s % SUBLANES, r0 + s - s % SUBLANES
            w_tile = cwb_ref[j]
            for g in range(len(accs)):
                rows = slice(base + g * SUBLANES, base + (g + 1) * SUBLANES)
                src = cext_ref[rows, :] if r == 0 else cshift_ref[r - 1, rows, :]
                accs[g] = src * w_tile if accs[g] is None else accs[g] + src * w_tile
        conv_ref[r0:r0 + CONV_ROWS, :] = jnp.concatenate(accs, axis=0)
        slab_end = (step + 1) * gate_slabs // n_conv_steps
        while slab < slab_end:
            lo = slab * MXU_DIM
            gates_ref[:, lo:lo + MXU_DIM] = jax.nn.sigmoid(
                proj(wwide_ref, SEG_GATES[0] + lo, SEG_GATES[0] + lo + MXU_DIM))
            slab += 1
    cext_ref[0:CONV_HALO, :] = cext_ref[tm:tm + CONV_HALO, :]

    cos = cos_ref[0]
    sin = sin_ref[0]

    cq = _rms(proj(wlat_ref, *SEG_Q), qn_ref[...]).astype(BF16)
    qt = _dot(wuqt_ref[...], cq, _NT)
    for h in range(N_HEADS):
        base = h * HEAD_PAD
        r1, r2 = _rope_rows(qt[base + NOPE:base + NOPE + ROPE_HALF], qt[base + NOPE + ROPE_HALF:base + QK_DIM],
                            cos, sin)
        blk = jnp.concatenate([qt[base:base + NOPE], r1, r2, qt[base + QK_DIM:base + HEAD_PAD]], axis=0)
        qt_ref[0, base:base + HEAD_PAD, :] = (blk * Q_SCALE).astype(BF16)

    ckv = _rms(proj(wlat_ref, *SEG_KV), kvn_ref[...]).astype(BF16)
    kpad = _dot(ckv, wuk_ref[...])
    row_id = lax.broadcasted_iota(jnp.int32, (N_HEADS * V_ROWS, 1), 0)
    ones_row = sum((row_id == h * V_ROWS + V_DIM).astype(F32) for h in range(N_HEADS))
    vt_ref[0] = (_dot(wuvt_ref[...], ckv, _NT) + ones_row).astype(BF16)
    krt = proj(wkr_ref, 0, HEAD_PAD).T
    r1, r2 = _rope_rows(krt[0:ROPE_HALF], krt[ROPE_HALF:ROPE], cos, sin)
    kr = jnp.concatenate([jnp.zeros((NOPE, tm), F32), r1, r2, jnp.zeros((HEAD_PAD - QK_DIM, tm), F32)], axis=0).T
    for h in range(N_HEADS):
        base = h * HEAD_PAD
        k_ref[0, :, base:base + HEAD_PAD] = (kpad[:, base:base + HEAD_PAD] + kr).astype(BF16)

    hc = conv_ref[...] + cb_ref[...]
    xc = hc - jnp.mean(hc, axis=-1, keepdims=True)
    ln = xc * lax.rsqrt(jnp.mean(xc * xc, axis=-1, keepdims=True) + EPS) * lng_ref[...] + lnb_ref[...]
    yconv = _dot((ln * jax.nn.sigmoid(ln)).astype(BF16), wco_ref[...])

    u = proj(wwide_ref, *SEG_POOL)
    pext_ref[POOL_HALO:POOL_HALO + tm, :] = u
    t1 = i * tm + 1 + lax.broadcasted_iota(jnp.int32, (tm, 1), 0)
    pool_rows = tm + POOL_HALO
    run = pext_ref[...]
    diffs = []
    for gi, w in enumerate(POOL_WINDOWS):
        run = run + pltpu.roll(run, w // 2, 0)
        tot = run[POOL_HALO:pool_rows, 0:POOL_GD]
        cnt = jnp.minimum(t1, w).astype(F32)
        diffs.append(tot / cnt - u[:, gi * POOL_GD:(gi + 1) * POOL_GD])
        run = run[:, POOL_GD:]
    pext_ref[0:POOL_HALO, :] = pext_ref[tm:tm + POOL_HALO, :]
    dmix = jnp.concatenate(diffs, axis=1).astype(BF16)
    mixed = _dot(dmix, pbd_ref[...]) * ps_ref[...]
    ypool = _dot(mixed.astype(BF16), wpo_ref[...])

    part = gates_ref[:, D_MODEL:2 * D_MODEL] * yconv + gates_ref[:, 2 * D_MODEL:3 * D_MODEL] * ypool
    part_ref[0] = part.astype(BF16)
    g0_ref[0] = gates_ref[:, 0:D_MODEL].astype(BF16)


def _layer_spec(stacked, l):
    zeros = (0,) * (stacked.ndim - 1)
    return pl.BlockSpec((None,) + stacked.shape[1:], lambda b, i: (l,) + zeros, pipeline_mode=pl.Buffered(1))


def _mixer_in(x, cos_t, sin_t, p, l, *, tm):
    b, s, d = x.shape
    nt = s // tm
    row_tile = lambda w: pl.BlockSpec((1, tm, w), lambda bi, i: (bi, i, 0))
    col_tile = lambda r: pl.BlockSpec((1, r, tm), lambda bi, i: (bi, 0, i))
    consts = [p["npre"], p["w_lat"], p["w_kr"], p["w_wide"], p["q_norm"], p["w_uq_t"], p["kv_norm"], p["w_uk"], p["w_uv_t"],
              p["conv_w"], p["conv_b"], p["ln_g"], p["ln_b"], p["w_conv_o"], p["pool_bd"], p["pool_scale"],
              p["w_pool_o"]]
    return pl.pallas_call(
        functools.partial(_mixer_in_kernel, tm=tm),
        grid=(b, nt),
        in_specs=[row_tile(d), col_tile(ROPE_HALF), col_tile(ROPE_HALF)] + [_layer_spec(c, l) for c in consts],
        out_specs=[col_tile(N_HEADS * HEAD_PAD), row_tile(N_HEADS * HEAD_PAD), col_tile(N_HEADS * V_ROWS),
                   row_tile(d), row_tile(d)],
        out_shape=[
            jax.ShapeDtypeStruct((b, N_HEADS * HEAD_PAD, s), BF16),
            jax.ShapeDtypeStruct((b, s, N_HEADS * HEAD_PAD), BF16),
            jax.ShapeDtypeStruct((b, N_HEADS * V_ROWS, s), BF16),
            jax.ShapeDtypeStruct((b, s, d), BF16),
            jax.ShapeDtypeStruct((b, s, d), BF16),
        ],
        scratch_shapes=[
            pltpu.VMEM((tm + CONV_HALO, CONV_CH), F32),
            pltpu.VMEM((SUBLANES - 1, tm + CONV_HALO - SUBLANES, CONV_CH), F32),
            pltpu.VMEM((CONV_W, SUBLANES, CONV_CH), F32),
            pltpu.VMEM((tm, CONV_CH), F32),
            pltpu.VMEM((tm + POOL_HALO, POOL_CH), F32),
            pltpu.VMEM((tm, N_BRANCHES * D_MODEL), F32),
        ],
        compiler_params=pltpu.CompilerParams(
            dimension_semantics=("arbitrary", "arbitrary"), vmem_limit_bytes=VMEM_LIMIT_BYTES),
        name="mixer_in",
    )(x, cos_t, sin_t, *consts)


def _attn_kernel(qt_ref, k_ref, vt_ref, ot_ref, s0_ref, s1_ref, m_ref, acc_ref, *, tq):
    i = pl.program_id(1)
    half = tq // 2
    m_ref[...] = jnp.full(m_ref.shape, -jnp.inf, F32)
    acc_ref[...] = jnp.zeros(acc_ref.shape, F32)

    def head(h):
        return slice(h * HEAD_PAD, (h + 1) * HEAD_PAD)

    def scores_into(s_ref, c):
        k0 = pl.multiple_of(c * tq, tq)
        for h in range(N_HEADS):
            s_ref[h] = _dot(k_ref[0, pl.ds(k0, tq), head(h)], qt_ref[0, head(h), :])

    def update(s_ref, c, k_lo, n_keys, q_lo, masked):
        k0 = pl.multiple_of(c * tq + k_lo, half)
        for h in range(N_HEADS):
            s = s_ref[h, k_lo:k_lo + n_keys, q_lo:tq]
            if masked:
                key = lax.broadcasted_iota(jnp.int32, (n_keys, 1), 0) + (k_lo - q_lo)
                qry = lax.broadcasted_iota(jnp.int32, (1, tq - q_lo), 1)
                s = jnp.where(key <= qry, s, -jnp.inf)
            m = m_ref[h, :, q_lo:tq]
            m_new = jnp.maximum(m, jnp.max(s, axis=0, keepdims=True))
            alpha = jnp.exp2(m - m_new)
            prob = jnp.exp2(s - m_new).astype(BF16)
            m_ref[h, :, q_lo:tq] = m_new
            vc = vt_ref[0, h * V_ROWS:(h + 1) * V_ROWS, pl.ds(k0, n_keys)]
            acc_ref[h, :, q_lo:tq] = alpha * acc_ref[h, :, q_lo:tq] + _dot(vc, prob)

    def diagonal(s_ref):
        update(s_ref, i, 0, half, 0, True)
        update(s_ref, i, half, half, half, True)

    def pair(p, carry):
        scores_into(s1_ref, 2 * p + 1)
        update(s0_ref, 2 * p, 0, tq, 0, False)
        scores_into(s0_ref, 2 * p + 2)
        update(s1_ref, 2 * p + 1, 0, tq, 0, False)
        return carry

    @pl.when(i == 0)
    def _():
        scores_into(s0_ref, 0)
        diagonal(s0_ref)

    @pl.when(i == 1)
    def _():
        scores_into(s0_ref, 0)
        scores_into(s1_ref, 1)
        update(s0_ref, 0, 0, tq, 0, False)
        diagonal(s1_ref)

    @pl.when(i >= 2)
    def _():
        scores_into(s0_ref, 0)
        pair(0, 0)

    lax.fori_loop(1, i // 2, pair, 0)

    @pl.when(jnp.logical_and(i >= 2, i % 2 == 0))
    def _():
        diagonal(s0_ref)

    @pl.when(jnp.logical_and(i >= 2, i % 2 == 1))
    def _():
        scores_into(s1_ref, i)
        update(s0_ref, i - 1, 0, tq, 0, False)
        diagonal(s1_ref)

    for h in range(N_HEADS):
        ot_ref[0, h * V_DIM:(h + 1) * V_DIM, :] = (acc_ref[h, 0:V_DIM, :] / acc_ref[h, V_DIM:V_DIM + 1, :]).astype(BF16)


def _attention(q_t, k, v_t, *, tq):
    b, _, s = q_t.shape
    tk = tq
    assert s % tq == 0
    return pl.pallas_call(
        functools.partial(_attn_kernel, tq=tq),
        grid=(b, s // tq),
        in_specs=[
            pl.BlockSpec((1, N_HEADS * HEAD_PAD, tq), lambda bi, i: (bi, 0, i)),
            pl.BlockSpec((1, s, N_HEADS * HEAD_PAD), lambda bi, i: (bi, 0, 0)),
            pl.BlockSpec((1, N_HEADS * V_ROWS, s), lambda bi, i: (bi, 0, 0)),
        ],
        out_specs=pl.BlockSpec((1, N_HEADS * V_DIM, tq), lambda bi, i: (bi, 0, i)),
        out_shape=jax.ShapeDtypeStruct((b, N_HEADS * V_DIM, s), BF16),
        scratch_shapes=[
            pltpu.VMEM((N_HEADS, tk, tq), F32),
            pltpu.VMEM((N_HEADS, tk, tq), F32),
            pltpu.VMEM((N_HEADS, 1, tq), F32),
            pltpu.VMEM((N_HEADS, V_ROWS, tq), F32),
        ],
        compiler_params=pltpu.CompilerParams(
            dimension_semantics=("arbitrary", "arbitrary"), vmem_limit_bytes=VMEM_LIMIT_BYTES),
        name="attention",
    )(q_t, k, v_t)


def _merge_ffn_kernel(h_ref, ot_ref, g0_ref, part_ref, wao_ref, wmo_ref, npost_ref, fpre_ref, wg_ref, wu_ref,
                      wd_ref, fpost_ref, out_ref):
    tm = h_ref.shape[1]
    groups = [slice(r, r + MERGE_ROWS) for r in range(0, tm, MERGE_ROWS)]
    h1 = []
    for rows in groups:
        yattn = _dot(ot_ref[0, :, rows], wao_ref[...], _TN)
        merged = g0_ref[0, rows, :].astype(F32) * yattn + part_ref[0, rows, :].astype(F32)
        h1.append(h_ref[0, rows, :] + _rms(_dot(merged.astype(BF16), wmo_ref[...]), npost_ref[...]))
    act = []
    for g in range(len(groups)):
        hn = _rms(h1[g], fpre_ref[...]).astype(BF16)
        gate = _dot(hn, wg_ref[...])
        act.append((gate * jax.nn.sigmoid(gate) * _dot(hn, wu_ref[...])).astype(BF16))
    for g, rows in enumerate(groups):
        out_ref[0, rows, :] = h1[g] + _rms(_dot(act[g], wd_ref[...]), fpost_ref[...])


def _merge_ffn(h, o_t, g0, part, p, l, *, tm):
    b, s, d = h.shape
    row_tile = lambda w: pl.BlockSpec((1, tm, w), lambda bi, i: (bi, i, 0))
    consts = [p["w_attn_o"], p["w_mix_o"], p["npost"], p["fpre"], p["w_gate"], p["w_up"], p["w_down"], p["fpost"]]
    return pl.pallas_call(
        _merge_ffn_kernel,
        grid=(b, s // tm),
        in_specs=[row_tile(d), pl.BlockSpec((1, N_HEADS * V_DIM, tm), lambda bi, i: (bi, 0, i)), row_tile(d),
                  row_tile(d)] + [_layer_spec(c, l) for c in consts],
        out_specs=row_tile(d),
        out_shape=jax.ShapeDtypeStruct((b, s, d), F32),
        compiler_params=pltpu.CompilerParams(
            dimension_semantics=("arbitrary", "arbitrary"), vmem_limit_bytes=VMEM_LIMIT_BYTES),
        name="merge_ffn",
    )(h, o_t, g0, part, *consts)


def _stacked_params(mix_norm_pre, w_in, q_norm, w_uq, kv_norm, w_uk, w_uv, w_attn_o, conv_w, conv_b, conv_ln_g,
                    conv_ln_b, w_conv_o, pool_w, pool_scale, w_pool_o, w_mix_o, mix_norm_post, ffn_norm_pre, w_gate,
                    w_up, w_down, ffn_norm_post):
    depth = w_in.shape[0]
    row = lambda v: v.reshape(depth, 1, -1)
    w_kr = jnp.pad(w_in[:, :, LAT_COLS:WIDE_START], ((0, 0), (0, 0), (0, HEAD_PAD - ROPE))).astype(BF16)
    assert w_in.shape[2] - WIDE_START == SEG_GATES[1]

    def pad_heads(w, dh, dh_pad):
        w = jnp.pad(w.reshape(depth, w.shape[1], N_HEADS, dh), ((0, 0), (0, 0), (0, 0), (0, dh_pad - dh)))
        return w.reshape(depth, w.shape[1], N_HEADS * dh_pad)

    transpose = lambda w: jnp.swapaxes(w, 1, 2)
    pool_bd = jnp.zeros((depth, POOL_CH, POOL_CH), F32)
    for gi in range(len(POOL_WINDOWS)):
        sl = slice(gi * POOL_GD, (gi + 1) * POOL_GD)
        pool_bd = pool_bd.at[:, sl, sl].set(pool_w[:, gi])
    return dict(
        npre=row(mix_norm_pre), w_lat=w_in[:, :, :LAT_COLS].astype(BF16), w_kr=w_kr,
        w_wide=w_in[:, :, WIDE_START:].astype(BF16), q_norm=row(q_norm),
        w_uq_t=transpose(pad_heads(w_uq, QK_DIM, HEAD_PAD)).astype(BF16), kv_norm=row(kv_norm),
        w_uk=pad_heads(w_uk, NOPE, HEAD_PAD).astype(BF16),
        w_uv_t=transpose(pad_heads(w_uv, V_DIM, V_ROWS)).astype(BF16),
        conv_w=conv_w, conv_b=row(conv_b), ln_g=row(conv_ln_g), ln_b=row(conv_ln_b),
        w_conv_o=w_conv_o.astype(BF16), pool_bd=pool_bd.astype(BF16), pool_scale=row(pool_scale),
        w_pool_o=w_pool_o.astype(BF16),
        w_attn_o=w_attn_o.astype(BF16), w_mix_o=w_mix_o.astype(BF16), npost=row(mix_norm_post),
        fpre=row(ffn_norm_pre), w_gate=w_gate.astype(BF16), w_up=w_up.astype(BF16),
        w_down=w_down.astype(BF16), fpost=row(ffn_norm_post),
    )


def kernel(x, positions, mix_norm_pre, w_in, q_norm, w_uq, kv_norm, w_uk, w_uv, w_attn_o, conv_w, conv_b, conv_ln_g, conv_ln_b, w_conv_o, pool_w, pool_scale, w_pool_o, w_mix_o, mix_norm_post, ffn_norm_pre, w_gate, w_up, w_down, ffn_norm_post):
    p = _stacked_params(mix_norm_pre, w_in, q_norm, w_uq, kv_norm, w_uk, w_uv, w_attn_o, conv_w, conv_b, conv_ln_g,
                        conv_ln_b, w_conv_o, pool_w, pool_scale, w_pool_o, w_mix_o, mix_norm_post, ffn_norm_pre,
                        w_gate, w_up, w_down, ffn_norm_post)
    s = x.shape[1]
    tm = min(ROW_TILE, s)
    cos_t, sin_t = _rope_tables(positions)
    h = x
    for l in range(w_in.shape[0]):
        q_t, k, v_t, g0, part = _mixer_in(h, cos_t, sin_t, p, l, tm=tm)
        o_t = _attention(q_t, k, v_t, tq=min(ATTN_TQ, s))
        h = _merge_ffn(h, o_t, g0, part, p, l, tm=tm)
    return h
```

```python
import functools
import math

import jax
import jax.numpy as jnp
from jax import lax
from jax.experimental import pallas as pl
from jax.experimental.pallas import tpu as pltpu

D_MODEL = 1024
N_HEADS = 8
NOPE = 64
ROPE = 32
ROPE_HALF = ROPE // 2
V_DIM = 64
Q_RANK = 384
KV_RANK = 256
ROPE_THETA = 10000.0
CONV_CH = 512
CONV_W = 31
POOL_WINDOWS = (2, 4, 8, 16)
POOL_CH = 512
POOL_GD = POOL_CH // len(POOL_WINDOWS)
N_BRANCHES = 3
D_FF = 2816
EPS = 1e-6

LANES = 128
SUBLANES = 8
BF16_ROWS = 16
MXU_DIM = 256
VMEM_LIMIT_BYTES = 56 * 1024 * 1024

HEAD_PAD = LANES
QK_DIM = NOPE + ROPE
V_ROWS = V_DIM + BF16_ROWS
Q_SCALE = (1.0 / math.sqrt(QK_DIM)) * math.log2(math.e)

LAT_COLS = Q_RANK + KV_RANK
WIDE_START = LAT_COLS + ROPE
SEG_Q = (0, Q_RANK)
SEG_KV = (Q_RANK, LAT_COLS)
SEG_CA = (0, 512)
SEG_CG = (512, 1024)
SEG_POOL = (1024, 1536)
SEG_GATES = (1536, 4608)

CONV_HALO = 32
POOL_HALO = 16
CONV_ROWS = 32

ROW_TILE = 512
MERGE_ROWS = 256
ATTN_TQ = 512

F32 = jnp.float32
BF16 = jnp.bfloat16

_NN = (((1,), (0,)), ((), ()))
_NT = (((1,), (1,)), ((), ()))
_TN = (((0,), (0,)), ((), ()))


def _rms(x, g):
    return x * lax.rsqrt(jnp.mean(x * x, axis=-1, keepdims=True) + EPS) * g


def _dot(a, b, dims=_NN):
    return lax.dot_general(a, b, dims, preferred_element_type=F32)


def _rope_rows(x1, x2, cos, sin):
    return x1 * cos - x2 * sin, x2 * cos + x1 * sin


def _rope_table_kernel(pos_ref, invf_ref, cos_ref, sin_ref):
    ang = invf_ref[...] * pos_ref[0].astype(F32)
    cos_ref[0] = jnp.cos(ang)
    sin_ref[0] = jnp.sin(ang)


def _rope_tables(positions):
    b, s = positions.shape
    inv_freq = ROPE_THETA ** (-jnp.arange(0, ROPE, 2, dtype=F32) / ROPE)
    out = jax.ShapeDtypeStruct((b, ROPE_HALF, s), F32)
    return pl.pallas_call(
        _rope_table_kernel,
        grid=(b,),
        in_specs=[
            pl.BlockSpec((1, 1, s), lambda i: (i, 0, 0)),
            pl.BlockSpec((ROPE_HALF, 1), lambda i: (0, 0)),
        ],
        out_specs=[
            pl.BlockSpec((1, ROPE_HALF, s), lambda i: (i, 0, 0)),
            pl.BlockSpec((1, ROPE_HALF, s), lambda i: (i, 0, 0)),
        ],
        out_shape=[out, out],
        name="rope_tables",
    )(positions.reshape(b, 1, s), inv_freq.reshape(ROPE_HALF, 1))


def _mixer_in_kernel(x_ref, cos_ref, sin_ref, npre_ref, wlat_ref, wkr_ref, wwide_ref, qn_ref, wuqt_ref, kvn_ref, wuk_ref, wuvt_ref,
                     cw_ref, cb_ref, lng_ref, lnb_ref, wco_ref, pbd_ref, ps_ref, wpo_ref,
                     qt_ref, k_ref, vt_ref, g0_ref, part_ref,
                     cext_ref, cshift_ref, cwb_ref, conv_ref, pext_ref, gates_ref, *, tm):
    i = pl.program_id(1)
    hn = [_rms(x_ref[0, r:r + MERGE_ROWS, :], npre_ref[...]).astype(BF16) for r in range(0, tm, MERGE_ROWS)]

    def proj(w_ref, lo, hi):
        return jnp.concatenate([_dot(g, w_ref[:, lo:hi]) for g in hn], axis=0)

    def rows_dot(a, w):
        return jnp.concatenate([_dot(a[r:r + MERGE_ROWS], w) for r in range(0, tm, MERGE_ROWS)], axis=0)

    @pl.when(i == 0)
    def _():
        cext_ref[0:CONV_HALO, :] = jnp.zeros((CONV_HALO, CONV_CH), F32)
        pext_ref[0:POOL_HALO, :] = jnp.zeros((POOL_HALO, POOL_CH), F32)
        for j in range(CONV_W):
            cwb_ref[j] = jnp.broadcast_to(cw_ref[j:j + 1, :], (SUBLANES, CONV_CH))

    cext_ref[CONV_HALO:CONV_HALO + tm, :] = proj(wwide_ref, *SEG_CA) * jax.nn.sigmoid(proj(wwide_ref, *SEG_CG))
    ext_rows = tm + CONV_HALO
    ext = cext_ref[...]
    for r in range(1, SUBLANES):
        cshift_ref[r - 1] = pltpu.roll(ext, ext_rows - r, 0)[0:ext_rows - SUBLANES]
    first_shift = CONV_HALO - (CONV_W - 1)
    n_conv_steps = tm // CONV_ROWS
    gate_slabs = (SEG_GATES[1] - SEG_GATES[0]) // MXU_DIM
    slab = 0
    for step in range(n_conv_steps):
        r0 = step * CONV_ROWS
        accs = [None] * (CONV_ROWS // SUBLANES)
        for j in range(CONV_W):
            s = first_shift + j
            r, base = s % SUBLANES, r0 + s - s % SUBLANES
            w_tile = cwb_ref[j]
            for g in range(len(accs)):
                rows = slice(base + g * SUBLANES, base + (g + 1) * SUBLANES)
                src = cext_ref[rows, :] if r == 0 else cshift_ref[r - 1, rows, :]
                accs[g] = src * w_tile if accs[g] is None else accs[g] + src * w_tile
        conv_ref[r0:r0 + CONV_ROWS, :] = jnp.concatenate(accs, axis=0)
        slab_end = (step + 1) * gate_slabs // n_conv_steps
        while slab < slab_end:
            lo = slab * MXU_DIM
            gates_ref[:, lo:lo + MXU_DIM] = jax.nn.sigmoid(
                proj(wwide_ref, SEG_GATES[0] + lo, SEG_GATES[0] + lo + MXU_DIM))
            slab += 1
    cext_ref[0:CONV_HALO, :] = cext_ref[tm:tm + CONV_HALO, :]

    cos = cos_ref[0]
    sin = sin_ref[0]

    cq = _rms(proj(wlat_ref, *SEG_Q), qn_ref[...]).astype(BF16)
    qt = _dot(wuqt_ref[...], cq, _NT)
    for h in range(N_HEADS):
        base = h * HEAD_PAD
        r1, r2 = _rope_rows(qt[base + NOPE:base + NOPE + ROPE_HALF], qt[base + NOPE + ROPE_HALF:base + QK_DIM],
                            cos, sin)
        blk = jnp.concatenate([qt[base:base + NOPE], r1, r2, qt[base + QK_DIM:base + HEAD_PAD]], axis=0)
        qt_ref[0, base:base + HEAD_PAD, :] = (blk * Q_SCALE).astype(BF16)

    ckv = _rms(proj(wlat_ref, *SEG_KV), kvn_ref[...]).astype(BF16)
    kpad = rows_dot(ckv, wuk_ref[...])
    row_id = lax.broadcasted_iota(jnp.int32, (N_HEADS * V_ROWS, 1), 0)
    ones_row = sum((row_id == h * V_ROWS + V_DIM).astype(F32) for h in range(N_HEADS))
    vt_ref[0] = (_dot(wuvt_ref[...], ckv, _NT) + ones_row).astype(BF16)
    krt = proj(wkr_ref, 0, HEAD_PAD).T
    r1, r2 = _rope_rows(krt[0:ROPE_HALF], krt[ROPE_HALF:ROPE], cos, sin)
    kr = jnp.concatenate([jnp.zeros((NOPE, tm), F32), r1, r2, jnp.zeros((HEAD_PAD - QK_DIM, tm), F32)], axis=0).T
    for h in range(N_HEADS):
        base = h * HEAD_PAD
        k_ref[0, :, base:base + HEAD_PAD] = (kpad[:, base:base + HEAD_PAD] + kr).astype(BF16)

    hc = conv_ref[...] + cb_ref[...]
    xc = hc - jnp.mean(hc, axis=-1, keepdims=True)
    ln = xc * lax.rsqrt(jnp.mean(xc * xc, axis=-1, keepdims=True) + EPS) * lng_ref[...] + lnb_ref[...]
    yconv = rows_dot((ln * jax.nn.sigmoid(ln)).astype(BF16), wco_ref[...])

    u = proj(wwide_ref, *SEG_POOL)
    pext_ref[POOL_HALO:POOL_HALO + tm, :] = u
    t1 = i * tm + 1 + lax.broadcasted_iota(jnp.int32, (tm, 1), 0)
    pool_rows = tm + POOL_HALO
    run = pext_ref[...]
    diffs = []
    for gi, w in enumerate(POOL_WINDOWS):
        run = run + pltpu.roll(run, w // 2, 0)
        tot = run[POOL_HALO:pool_rows, 0:POOL_GD]
        cnt = jnp.minimum(t1, w).astype(F32)
        diffs.append(tot / cnt - u[:, gi * POOL_GD:(gi + 1) * POOL_GD])
        run = run[:, POOL_GD:]
    pext_ref[0:POOL_HALO, :] = pext_ref[tm:tm + POOL_HALO, :]
    dmix = jnp.concatenate(diffs, axis=1).astype(BF16)
    mixed = rows_dot(dmix, pbd_ref[...]) * ps_ref[...]
    ypool = rows_dot(mixed.astype(BF16), wpo_ref[...])

    part = gates_ref[:, D_MODEL:2 * D_MODEL] * yconv + gates_ref[:, 2 * D_MODEL:3 * D_MODEL] * ypool
    part_ref[0] = part.astype(BF16)
    g0_ref[0] = gates_ref[:, 0:D_MODEL].astype(BF16)


def _layer_spec(stacked, l):
    zeros = (0,) * (stacked.ndim - 1)
    return pl.BlockSpec((None,) + stacked.shape[1:], lambda b, i: (l,) + zeros, pipeline_mode=pl.Buffered(1))


def _mixer_in(x, cos_t, sin_t, p, l, *, tm):
    b, s, d = x.shape
    nt = s // tm
    row_tile = lambda w: pl.BlockSpec((1, tm, w), lambda bi, i: (bi, i, 0))
    col_tile = lambda r: pl.BlockSpec((1, r, tm), lambda bi, i: (bi, 0, i))
    consts = [p["npre"], p["w_lat"], p["w_kr"], p["w_wide"], p["q_norm"], p["w_uq_t"], p["kv_norm"], p["w_uk"], p["w_uv_t"],
              p["conv_w"], p["conv_b"], p["ln_g"], p["ln_b"], p["w_conv_o"], p["pool_bd"], p["pool_scale"],
              p["w_pool_o"]]
    return pl.pallas_call(
        functools.partial(_mixer_in_kernel, tm=tm),
        grid=(b, nt),
        in_specs=[row_tile(d), col_tile(ROPE_HALF), col_tile(ROPE_HALF)] + [_layer_spec(c, l) for c in consts],
        out_specs=[col_tile(N_HEADS * HEAD_PAD), row_tile(N_HEADS * HEAD_PAD), col_tile(N_HEADS * V_ROWS),
                   row_tile(d), row_tile(d)],
        out_shape=[
            jax.ShapeDtypeStruct((b, N_HEADS * HEAD_PAD, s), BF16),
            jax.ShapeDtypeStruct((b, s, N_HEADS * HEAD_PAD), BF16),
            jax.ShapeDtypeStruct((b, N_HEADS * V_ROWS, s), BF16),
            jax.ShapeDtypeStruct((b, s, d), BF16),
            jax.ShapeDtypeStruct((b, s, d), BF16),
        ],
        scratch_shapes=[
            pltpu.VMEM((tm + CONV_HALO, CONV_CH), F32),
            pltpu.VMEM((SUBLANES - 1, tm + CONV_HALO - SUBLANES, CONV_CH), F32),
            pltpu.VMEM((CONV_W, SUBLANES, CONV_CH), F32),
            pltpu.VMEM((tm, CONV_CH), F32),
            pltpu.VMEM((tm + POOL_HALO, POOL_CH), F32),
            pltpu.VMEM((tm, N_BRANCHES * D_MODEL), F32),
        ],
        compiler_params=pltpu.CompilerParams(
            dimension_semantics=("arbitrary", "arbitrary"), vmem_limit_bytes=VMEM_LIMIT_BYTES),
        name="mixer_in",
    )(x, cos_t, sin_t, *consts)


def _attn_kernel(qt_ref, k_ref, vt_ref, ot_ref, s0_ref, s1_ref, m_ref, acc_ref, *, tq):
    i = pl.program_id(1)
    half = tq // 2
    m_ref[...] = jnp.full(m_ref.shape, -jnp.inf, F32)
    acc_ref[...] = jnp.zeros(acc_ref.shape, F32)

    def head(h):
        return slice(h * HEAD_PAD, (h + 1) * HEAD_PAD)

    def scores_into(s_ref, c):
        k0 = pl.multiple_of(c * tq, tq)
        for h in range(N_HEADS):
            s_ref[h] = _dot(k_ref[0, pl.ds(k0, tq), head(h)], qt_ref[0, head(h), :])

    def update(s_ref, c, k_lo, n_keys, q_lo, masked):
        k0 = pl.multiple_of(c * tq + k_lo, half)
        for h in range(N_HEADS):
            s = s_ref[h, k_lo:k_lo + n_keys, q_lo:tq]
            if masked:
                key = lax.broadcasted_iota(jnp.int32, (n_keys, 1), 0) + (k_lo - q_lo)
                qry = lax.broadcasted_iota(jnp.int32, (1, tq - q_lo), 1)
                s = jnp.where(key <= qry, s, -jnp.inf)
            m = m_ref[h, :, q_lo:tq]
            m_new = jnp.maximum(m, jnp.max(s, axis=0, keepdims=True))
            alpha = jnp.exp2(m - m_new)
            prob = jnp.exp2(s - m_new).astype(BF16)
            m_ref[h, :, q_lo:tq] = m_new
            vc = vt_ref[0, h * V_ROWS:(h + 1) * V_ROWS, pl.ds(k0, n_keys)]
            acc_ref[h, :, q_lo:tq] = alpha * acc_ref[h, :, q_lo:tq] + _dot(vc, prob)

    def diagonal(s_ref):
        update(s_ref, i, 0, half, 0, True)
        update(s_ref, i, half, half, half, True)

    def pair(p, carry):
        scores_into(s1_ref, 2 * p + 1)
        update(s0_ref, 2 * p, 0, tq, 0, False)
        scores_into(s0_ref, 2 * p + 2)
        update(s1_ref, 2 * p + 1, 0, tq, 0, False)
        return carry

    @pl.when(i == 0)
    def _():
        scores_into(s0_ref, 0)
        diagonal(s0_ref)

    @pl.when(i == 1)
    def _():
        scores_into(s0_ref, 0)
        scores_into(s1_ref, 1)
        update(s0_ref, 0, 0, tq, 0, False)
        diagonal(s1_ref)

    @pl.when(i >= 2)
    def _():
        scores_into(s0_ref, 0)
        pair(0, 0)

    lax.fori_loop(1, i // 2, pair, 0)

    @pl.when(jnp.logical_and(i >= 2, i % 2 == 0))
    def _():
        diagonal(s0_ref)

    @pl.when(jnp.logical_and(i >= 2, i % 2 == 1))
    def _():
        scores_into(s1_ref, i)
        update(s0_ref, i - 1, 0, tq, 0, False)
        diagonal(s1_ref)

    for h in range(N_HEADS):
        ot_ref[0, h * V_DIM:(h + 1) * V_DIM, :] = (acc_ref[h, 0:V_DIM, :] / acc_ref[h, V_DIM:V_DIM + 1, :]).astype(BF16)


def _attention(q_t, k, v_t, *, tq):
    b, _, s = q_t.shape
    tk = tq
    assert s % tq == 0
    return pl.pallas_call(
        functools.partial(_attn_kernel, tq=tq),
        grid=(b, s // tq),
        in_specs=[
            pl.BlockSpec((1, N_HEADS * HEAD_PAD, tq), lambda bi, i: (bi, 0, i)),
            pl.BlockSpec((1, s, N_HEADS * HEAD_PAD), lambda bi, i: (bi, 0, 0)),
            pl.BlockSpec((1, N_HEADS * V_ROWS, s), lambda bi, i: (bi, 0, 0)),
        ],
        out_specs=pl.BlockSpec((1, N_HEADS * V_DIM, tq), lambda bi, i: (bi, 0, i)),
        out_shape=jax.ShapeDtypeStruct((b, N_HEADS * V_DIM, s), BF16),
        scratch_shapes=[
            pltpu.VMEM((N_HEADS, tk, tq), F32),
            pltpu.VMEM((N_HEADS, tk, tq), F32),
            pltpu.VMEM((N_HEADS, 1, tq), F32),
            pltpu.VMEM((N_HEADS, V_ROWS, tq), F32),
        ],
        compiler_params=pltpu.CompilerParams(
            dimension_semantics=("arbitrary", "arbitrary"), vmem_limit_bytes=VMEM_LIMIT_BYTES),
        name="attention",
    )(q_t, k, v_t)


def _merge_ffn_kernel(h_ref, ot_ref, g0_ref, part_ref, wao_ref, wmo_ref, npost_ref, fpre_ref, wg_ref, wu_ref,
                      wd_ref, fpost_ref, out_ref):
    tm = h_ref.shape[1]
    groups = [slice(r, r + MERGE_ROWS) for r in range(0, tm, MERGE_ROWS)]
    h1 = []
    for rows in groups:
        yattn = _dot(ot_ref[0, :, rows], wao_ref[...], _TN)
        merged = g0_ref[0, rows, :].astype(F32) * yattn + part_ref[0, rows, :].astype(F32)
        h1.append(h_ref[0, rows, :] + _rms(_dot(merged.astype(BF16), wmo_ref[...]), npost_ref[...]))
    act = []
    for g in range(len(groups)):
        hn = _rms(h1[g], fpre_ref[...]).astype(BF16)
        gate = _dot(hn, wg_ref[...])
        act.append((gate * jax.nn.sigmoid(gate) * _dot(hn, wu_ref[...])).astype(BF16))
    for g, rows in enumerate(groups):
        out_ref[0, rows, :] = h1[g] + _rms(_dot(act[g], wd_ref[...]), fpost_ref[...])


def _merge_ffn(h, o_t, g0, part, p, l, *, tm):
    b, s, d = h.shape
    row_tile = lambda w: pl.BlockSpec((1, tm, w), lambda bi, i: (bi, i, 0))
    consts = [p["w_attn_o"], p["w_mix_o"], p["npost"], p["fpre"], p["w_gate"], p["w_up"], p["w_down"], p["fpost"]]
    return pl.pallas_call(
        _merge_ffn_kernel,
        grid=(b, s // tm),
        in_specs=[row_tile(d), pl.BlockSpec((1, N_HEADS * V_DIM, tm), lambda bi, i: (bi, 0, i)), row_tile(d),
                  row_tile(d)] + [_layer_spec(c, l) for c in consts],
        out_specs=row_tile(d),
        out_shape=jax.ShapeDtypeStruct((b, s, d), F32),
        compiler_params=pltpu.CompilerParams(
            dimension_semantics=("arbitrary", "arbitrary"), vmem_limit_bytes=VMEM_LIMIT_BYTES),
        name="merge_ffn",
    )(h, o_t, g0, part, *consts)


def _stacked_params(mix_norm_pre, w_in, q_norm, w_uq, kv_norm, w_uk, w_uv, w_attn_o, conv_w, conv_b, conv_ln_g,
                    conv_ln_b, w_conv_o, pool_w, pool_scale, w_pool_o, w_mix_o, mix_norm_post, ffn_norm_pre, w_gate,
                    w_up, w_down, ffn_norm_post):
    depth = w_in.shape[0]
    row = lambda v: v.reshape(depth, 1, -1)
    w_in = w_in.astype(BF16)
    w_kr = jnp.pad(w_in[:, :, LAT_COLS:WIDE_START], ((0, 0), (0, 0), (0, HEAD_PAD - ROPE)))
    assert w_in.shape[2] - WIDE_START == SEG_GATES[1]

    def pad_heads(w, dh, dh_pad):
        w = jnp.pad(w.reshape(depth, w.shape[1], N_HEADS, dh), ((0, 0), (0, 0), (0, 0), (0, dh_pad - dh)))
        return w.reshape(depth, w.shape[1], N_HEADS * dh_pad)

    transpose = lambda w: jnp.swapaxes(w, 1, 2)
    pool_bd = jnp.zeros((depth, POOL_CH, POOL_CH), F32)
    for gi in range(len(POOL_WINDOWS)):
        sl = slice(gi * POOL_GD, (gi + 1) * POOL_GD)
        pool_bd = pool_bd.at[:, sl, sl].set(pool_w[:, gi])
    return dict(
        npre=row(mix_norm_pre), w_lat=w_in[:, :, :LAT_COLS], w_kr=w_kr,
        w_wide=w_in[:, :, WIDE_START:], q_norm=row(q_norm),
        w_uq_t=transpose(pad_heads(w_uq, QK_DIM, HEAD_PAD)).astype(BF16), kv_norm=row(kv_norm),
        w_uk=pad_heads(w_uk, NOPE, HEAD_PAD).astype(BF16),
        w_uv_t=transpose(pad_heads(w_uv, V_DIM, V_ROWS)).astype(BF16),
        conv_w=conv_w, conv_b=row(conv_b), ln_g=row(conv_ln_g), ln_b=row(conv_ln_b),
        w_conv_o=w_conv_o.astype(BF16), pool_bd=pool_bd.astype(BF16), pool_scale=row(pool_scale),
        w_pool_o=w_pool_o.astype(BF16),
        w_attn_o=w_attn_o.astype(BF16), w_mix_o=w_mix_o.astype(BF16), npost=row(mix_norm_post),
        fpre=row(ffn_norm_pre), w_gate=w_gate.astype(BF16), w_up=w_up.astype(BF16),
        w_down=w_down.astype(BF16), fpost=row(ffn_norm_post),
    )


def kernel(x, positions, mix_norm_pre, w_in, q_norm, w_uq, kv_norm, w_uk, w_uv, w_attn_o, conv_w, conv_b, conv_ln_g, conv_ln_b, w_conv_o, pool_w, pool_scale, w_pool_o, w_mix_o, mix_norm_post, ffn_norm_pre, w_gate, w_up, w_down, ffn_norm_post):
    p = _stacked_params(mix_norm_pre, w_in, q_norm, w_uq, kv_norm, w_uk, w_uv, w_attn_o, conv_w, conv_b, conv_ln_g,
                        conv_ln_b, w_conv_o, pool_w, pool_scale, w_pool_o, w_mix_o, mix_norm_post, ffn_norm_pre,
                        w_gate, w_up, w_down, ffn_norm_post)
    s = x.shape[1]
    tm = min(ROW_TILE, s)
    cos_t, sin_t = _rope_tables(positions)
    h = x
    for l in range(w_in.shape[0]):
        q_t, k, v_t, g0, part = _mixer_in(h, cos_t, sin_t, p, l, tm=tm)
        o_t = _attention(q_t, k, v_t, tq=min(ATTN_TQ, s))
        h = _merge_ffn(h, o_t, g0, part, p, l, tm=tm)
    return h
```

```python
import functools
import math

import jax
import jax.numpy as jnp
from jax import lax
from jax.experimental import pallas as pl
from jax.experimental.pallas import tpu as pltpu

D_MODEL = 1024
N_HEADS = 8
NOPE = 64
ROPE = 32
ROPE_HALF = ROPE // 2
V_DIM = 64
Q_RANK = 384
KV_RANK = 256
ROPE_THETA = 10000.0
CONV_CH = 512
CONV_W = 31
POOL_WINDOWS = (2, 4, 8, 16)
POOL_CH = 512
POOL_GD = POOL_CH // len(POOL_WINDOWS)
N_BRANCHES = 3
D_FF = 2816
EPS = 1e-6

LANES = 128
SUBLANES = 8
BF16_ROWS = 16
MXU_DIM = 256
VMEM_LIMIT_BYTES = 56 * 1024 * 1024

HEAD_PAD = LANES
QK_DIM = NOPE + ROPE
V_ROWS = V_DIM + BF16_ROWS
Q_SCALE = (1.0 / math.sqrt(QK_DIM)) * math.log2(math.e)

LAT_COLS = Q_RANK + KV_RANK
WIDE_START = LAT_COLS + ROPE
SEG_Q = (0, Q_RANK)
SEG_KV = (Q_RANK, LAT_COLS)
SEG_CA = (0, CONV_CH)
SEG_CG = (CONV_CH, 2 * CONV_CH)
SEG_POOL = (2 * CONV_CH, 2 * CONV_CH + POOL_CH)
SEG_GATES = (SEG_POOL[1], SEG_POOL[1] + N_BRANCHES * D_MODEL)

CONV_HALO = 32
POOL_HALO = 16
CONV_ROWS = 32

ROW_TILE = 512
MERGE_ROWS = 256
ATTN_TQ = 512

F32 = jnp.float32
BF16 = jnp.bfloat16

_NN = (((1,), (0,)), ((), ()))
_NT = (((1,), (1,)), ((), ()))
_TN = (((0,), (0,)), ((), ()))


def _rms(x, g):
    return x * lax.rsqrt(jnp.mean(x * x, axis=-1, keepdims=True) + EPS) * g


def _dot(a, b, dims=_NN):
    return lax.dot_general(a, b, dims, preferred_element_type=F32)


def _rope_rows(x1, x2, cos, sin):
    return x1 * cos - x2 * sin, x2 * cos + x1 * sin


def _rope_table_kernel(pos_ref, invf_ref, cos_ref, sin_ref):
    ang = invf_ref[...] * pos_ref[0].astype(F32)
    cos_ref[0] = jnp.cos(ang)
    sin_ref[0] = jnp.sin(ang)


def _rope_tables(positions):
    b, s = positions.shape
    inv_freq = ROPE_THETA ** (-jnp.arange(0, ROPE, 2, dtype=F32) / ROPE)
    out = jax.ShapeDtypeStruct((b, ROPE_HALF, s), F32)
    return pl.pallas_call(
        _rope_table_kernel,
        grid=(b,),
        in_specs=[
            pl.BlockSpec((1, 1, s), lambda i: (i, 0, 0)),
            pl.BlockSpec((ROPE_HALF, 1), lambda i: (0, 0)),
        ],
        out_specs=[
            pl.BlockSpec((1, ROPE_HALF, s), lambda i: (i, 0, 0)),
            pl.BlockSpec((1, ROPE_HALF, s), lambda i: (i, 0, 0)),
        ],
        out_shape=[out, out],
        name="rope_tables",
    )(positions.reshape(b, 1, s), inv_freq.reshape(ROPE_HALF, 1))


def _mixer_in_kernel(x_ref, cos_ref, sin_ref, npre_ref, wlat_ref, wkr_ref, wwide_ref, qn_ref, wuqt_ref, kvn_ref,
                     wuk_ref, wuvt_ref, cw_ref, cb_ref, lng_ref, lnb_ref, wco_ref, pbd_ref, ps_ref, wpo_ref,
                     qt_ref, k_ref, vt_ref, g0_ref, part_ref,
                     cext_ref, cshift_ref, cwb_ref, conv_ref, pext_ref, gates_ref, *, tm):
    i = pl.program_id(1)
    hn = [_rms(x_ref[0, r:r + MERGE_ROWS, :], npre_ref[...]).astype(BF16) for r in range(0, tm, MERGE_ROWS)]

    def proj(w_ref, lo, hi):
        return jnp.concatenate([_dot(g, w_ref[:, lo:hi]) for g in hn], axis=0)

    def rows_dot(a, w):
        return jnp.concatenate([_dot(a[r:r + MERGE_ROWS], w) for r in range(0, tm, MERGE_ROWS)], axis=0)

    @pl.when(i == 0)
    def _():
        cext_ref[0:CONV_HALO, :] = jnp.zeros((CONV_HALO, CONV_CH), F32)
        pext_ref[0:POOL_HALO, :] = jnp.zeros((POOL_HALO, POOL_CH), F32)
        for j in range(CONV_W):
            cwb_ref[j] = jnp.broadcast_to(cw_ref[j:j + 1, :], (SUBLANES, CONV_CH))

    cext_ref[CONV_HALO:CONV_HALO + tm, :] = proj(wwide_ref, *SEG_CA) * jax.nn.sigmoid(proj(wwide_ref, *SEG_CG))
    ext_rows = tm + CONV_HALO
    ext = cext_ref[...]
    for r in range(1, SUBLANES):
        cshift_ref[r - 1] = pltpu.roll(ext, ext_rows - r, 0)[0:ext_rows - SUBLANES]
    first_shift = CONV_HALO - (CONV_W - 1)
    n_conv_steps = tm // CONV_ROWS
    gate_slabs = (SEG_GATES[1] - SEG_GATES[0]) // MXU_DIM
    slab = 0
    for step in range(n_conv_steps):
        r0 = step * CONV_ROWS
        accs = [None] * (CONV_ROWS // SUBLANES)
        for j in range(CONV_W):
            s = first_shift + j
            r, base = s % SUBLANES, r0 + s - s % SUBLANES
            w_tile = cwb_ref[j]
            for g in range(len(accs)):
                rows = slice(base + g * SUBLANES, base + (g + 1) * SUBLANES)
                src = cext_ref[rows, :] if r == 0 else cshift_ref[r - 1, rows, :]
                accs[g] = src * w_tile if accs[g] is None else accs[g] + src * w_tile
        conv_ref[r0:r0 + CONV_ROWS, :] = jnp.concatenate(accs, axis=0)
        slab_end = (step + 1) * gate_slabs // n_conv_steps
        while slab < slab_end:
            lo = slab * MXU_DIM
            gates_ref[:, lo:lo + MXU_DIM] = jax.nn.sigmoid(
                proj(wwide_ref, SEG_GATES[0] + lo, SEG_GATES[0] + lo + MXU_DIM))
            slab += 1
    cext_ref[0:CONV_HALO, :] = cext_ref[tm:tm + CONV_HALO, :]

    cos = cos_ref[0]
    sin = sin_ref[0]

    cq = _rms(proj(wlat_ref, *SEG_Q), qn_ref[...]).astype(BF16)
    qt = _dot(wuqt_ref[...], cq, _NT)
    for h in range(N_HEADS):
        base = h * HEAD_PAD
        r1, r2 = _rope_rows(qt[base + NOPE:base + NOPE + ROPE_HALF], qt[base + NOPE + ROPE_HALF:base + QK_DIM],
                            cos, sin)
        blk = jnp.concatenate([qt[base:base + NOPE], r1, r2, qt[base + QK_DIM:base + HEAD_PAD]], axis=0)
        qt_ref[0, base:base + HEAD_PAD, :] = (blk * Q_SCALE).astype(BF16)

    ckv = _rms(proj(wlat_ref, *SEG_KV), kvn_ref[...]).astype(BF16)
    kpad = rows_dot(ckv, wuk_ref[...])
    row_id = lax.broadcasted_iota(jnp.int32, (N_HEADS * V_ROWS, 1), 0)
    ones_row = sum((row_id == h * V_ROWS + V_DIM).astype(F32) for h in range(N_HEADS))
    vt_ref[0] = (_dot(wuvt_ref[...], ckv, _NT) + ones_row).astype(BF16)
    krt = proj(wkr_ref, 0, HEAD_PAD).T
    r1, r2 = _rope_rows(krt[0:ROPE_HALF], krt[ROPE_HALF:ROPE], cos, sin)
    kr = jnp.concatenate([jnp.zeros((NOPE, tm), F32), r1, r2, jnp.zeros((HEAD_PAD - QK_DIM, tm), F32)], axis=0).T
    for h in range(N_HEADS):
        base = h * HEAD_PAD
        k_ref[0, :, base:base + HEAD_PAD] = (kpad[:, base:base + HEAD_PAD] + kr).astype(BF16)

    hc = conv_ref[...] + cb_ref[...]
    xc = hc - jnp.mean(hc, axis=-1, keepdims=True)
    ln = xc * lax.rsqrt(jnp.mean(xc * xc, axis=-1, keepdims=True) + EPS) * lng_ref[...] + lnb_ref[...]
    yconv = rows_dot((ln * jax.nn.sigmoid(ln)).astype(BF16), wco_ref[...])

    u = proj(wwide_ref, *SEG_POOL)
    pext_ref[POOL_HALO:POOL_HALO + tm, :] = u
    t1 = i * tm + 1 + lax.broadcasted_iota(jnp.int32, (tm, 1), 0)
    pool_rows = tm + POOL_HALO
    run = pext_ref[...]
    diffs = []
    for gi, w in enumerate(POOL_WINDOWS):
        run = run + pltpu.roll(run, w // 2, 0)
        tot = run[POOL_HALO:pool_rows, 0:POOL_GD]
        cnt = jnp.minimum(t1, w).astype(F32)
        diffs.append(tot / cnt - u[:, gi * POOL_GD:(gi + 1) * POOL_GD])
        run = run[:, POOL_GD:]
    pext_ref[0:POOL_HALO, :] = pext_ref[tm:tm + POOL_HALO, :]
    dmix = jnp.concatenate(diffs, axis=1).astype(BF16)
    mixed = rows_dot(dmix, pbd_ref[...]) * ps_ref[...]
    ypool = rows_dot(mixed.astype(BF16), wpo_ref[...])

    part = gates_ref[:, D_MODEL:2 * D_MODEL] * yconv + gates_ref[:, 2 * D_MODEL:3 * D_MODEL] * ypool
    part_ref[0] = part.astype(BF16)
    g0_ref[0] = gates_ref[:, 0:D_MODEL].astype(BF16)


def _layer_spec(stacked, l):
    zeros = (0,) * (stacked.ndim - 1)
    return pl.BlockSpec((None,) + stacked.shape[1:], lambda b, i: (l,) + zeros, pipeline_mode=pl.Buffered(1))


def _mixer_in(x, cos_t, sin_t, p, l, *, tm):
    b, s, d = x.shape
    nt = s // tm
    row_tile = lambda w: pl.BlockSpec((1, tm, w), lambda bi, i: (bi, i, 0))
    col_tile = lambda r: pl.BlockSpec((1, r, tm), lambda bi, i: (bi, 0, i))
    assert s % tm == 0 and tm % MERGE_ROWS == 0 and tm % CONV_ROWS == 0
    consts = [p[name] for name in (
        "npre", "w_lat", "w_kr", "w_wide", "q_norm", "w_uq_t", "kv_norm", "w_uk", "w_uv_t", "conv_w", "conv_b",
        "ln_g", "ln_b", "w_conv_o", "pool_bd", "pool_scale", "w_pool_o")]
    return pl.pallas_call(
        functools.partial(_mixer_in_kernel, tm=tm),
        grid=(b, nt),
        in_specs=[row_tile(d), col_tile(ROPE_HALF), col_tile(ROPE_HALF)] + [_layer_spec(c, l) for c in consts],
        out_specs=[col_tile(N_HEADS * HEAD_PAD), row_tile(N_HEADS * HEAD_PAD), col_tile(N_HEADS * V_ROWS),
                   row_tile(d), row_tile(d)],
        out_shape=[
            jax.ShapeDtypeStruct((b, N_HEADS * HEAD_PAD, s), BF16),
            jax.ShapeDtypeStruct((b, s, N_HEADS * HEAD_PAD), BF16),
            jax.ShapeDtypeStruct((b, N_HEADS * V_ROWS, s), BF16),
            jax.ShapeDtypeStruct((b, s, d), BF16),
            jax.ShapeDtypeStruct((b, s, d), BF16),
        ],
        scratch_shapes=[
            pltpu.VMEM((tm + CONV_HALO, CONV_CH), F32),
            pltpu.VMEM((SUBLANES - 1, tm + CONV_HALO - SUBLANES, CONV_CH), F32),
            pltpu.VMEM((CONV_W, SUBLANES, CONV_CH), F32),
            pltpu.VMEM((tm, CONV_CH), F32),
            pltpu.VMEM((tm + POOL_HALO, POOL_CH), F32),
            pltpu.VMEM((tm, N_BRANCHES * D_MODEL), F32),
        ],
        compiler_params=pltpu.CompilerParams(
            dimension_semantics=("arbitrary", "arbitrary"), vmem_limit_bytes=VMEM_LIMIT_BYTES),
        name="mixer_in",
    )(x, cos_t, sin_t, *consts)


def _attn_kernel(qt_ref, k_ref, vt_ref, ot_ref, s0_ref, s1_ref, m_ref, acc_ref, *, tq):
    i = pl.program_id(1)
    half = tq // 2
    m_ref[...] = jnp.full(m_ref.shape, -jnp.inf, F32)
    acc_ref[...] = jnp.zeros(acc_ref.shape, F32)

    def head(h):
        return slice(h * HEAD_PAD, (h + 1) * HEAD_PAD)

    def scores_into(s_ref, c):
        k0 = pl.multiple_of(c * tq, tq)
        for h in range(N_HEADS):
            s_ref[h] = _dot(k_ref[0, pl.ds(k0, tq), head(h)], qt_ref[0, head(h), :])

    def update(s_ref, c, k_lo, n_keys, q_lo, masked):
        k0 = pl.multiple_of(c * tq + k_lo, half)
        for h in range(N_HEADS):
            s = s_ref[h, k_lo:k_lo + n_keys, q_lo:tq]
            if masked:
                key = lax.broadcasted_iota(jnp.int32, (n_keys, 1), 0) + (k_lo - q_lo)
                qry = lax.broadcasted_iota(jnp.int32, (1, tq - q_lo), 1)
                s = jnp.where(key <= qry, s, -jnp.inf)
            m = m_ref[h, :, q_lo:tq]
            m_new = jnp.maximum(m, jnp.max(s, axis=0, keepdims=True))
            alpha = jnp.exp2(m - m_new)
            prob = jnp.exp2(s - m_new).astype(BF16)
            m_ref[h, :, q_lo:tq] = m_new
            vc = vt_ref[0, h * V_ROWS:(h + 1) * V_ROWS, pl.ds(k0, n_keys)]
            acc_ref[h, :, q_lo:tq] = alpha * acc_ref[h, :, q_lo:tq] + _dot(vc, prob)

    def diagonal(s_ref):
        update(s_ref, i, 0, half, 0, True)
        update(s_ref, i, half, half, half, True)

    def pair(p, carry):
        scores_into(s1_ref, 2 * p + 1)
        update(s0_ref, 2 * p, 0, tq, 0, False)
        scores_into(s0_ref, 2 * p + 2)
        update(s1_ref, 2 * p + 1, 0, tq, 0, False)
        return carry

    def tail_even():
        diagonal(s0_ref)

    def tail_odd():
        scores_into(s1_ref, i)
        update(s0_ref, i - 1, 0, tq, 0, False)
        diagonal(s1_ref)

    @pl.when(i == 0)
    def _():
        scores_into(s0_ref, 0)
        tail_even()

    @pl.when(i == 1)
    def _():
        scores_into(s0_ref, 0)
        tail_odd()

    @pl.when(i == 2)
    def _():
        scores_into(s0_ref, 0)
        pair(0, 0)
        tail_even()

    @pl.when(i == 3)
    def _():
        scores_into(s0_ref, 0)
        pair(0, 0)
        tail_odd()

    @pl.when(i >= 4)
    def _():
        scores_into(s0_ref, 0)
        pair(0, 0)

    lax.fori_loop(1, i // 2 - 1, pair, 0)

    @pl.when(jnp.logical_and(i >= 4, i % 2 == 0))
    def _():
        pair(i // 2 - 1, 0)
        tail_even()

    @pl.when(jnp.logical_and(i >= 4, i % 2 == 1))
    def _():
        pair(i // 2 - 1, 0)
        tail_odd()

    for h in range(N_HEADS):
        ot_ref[0, h * V_DIM:(h + 1) * V_DIM, :] = (acc_ref[h, 0:V_DIM, :] / acc_ref[h, V_DIM:V_DIM + 1, :]).astype(BF16)


def _attention(q_t, k, v_t, *, tq):
    b, _, s = q_t.shape
    tk = tq
    assert s % tq == 0
    return pl.pallas_call(
        functools.partial(_attn_kernel, tq=tq),
        grid=(b, s // tq),
        in_specs=[
            pl.BlockSpec((1, N_HEADS * HEAD_PAD, tq), lambda bi, i: (bi, 0, i)),
            pl.BlockSpec((1, s, N_HEADS * HEAD_PAD), lambda bi, i: (bi, 0, 0)),
            pl.BlockSpec((1, N_HEADS * V_ROWS, s), lambda bi, i: (bi, 0, 0), pipeline_mode=pl.Buffered(1)),
        ],
        out_specs=pl.BlockSpec((1, N_HEADS * V_DIM, tq), lambda bi, i: (bi, 0, i)),
        out_shape=jax.ShapeDtypeStruct((b, N_HEADS * V_DIM, s), BF16),
        scratch_shapes=[
            pltpu.VMEM((N_HEADS, tk, tq), F32),
            pltpu.VMEM((N_HEADS, tk, tq), F32),
            pltpu.VMEM((N_HEADS, 1, tq), F32),
            pltpu.VMEM((N_HEADS, V_ROWS, tq), F32),
        ],
        compiler_params=pltpu.CompilerParams(
            dimension_semantics=("arbitrary", "arbitrary"), vmem_limit_bytes=VMEM_LIMIT_BYTES),
        name="attention",
    )(q_t, k, v_t)


def _merge_ffn_kernel(h_ref, ot_ref, g0_ref, part_ref, wao_ref, wmo_ref, npost_ref, fpre_ref, wg_ref, wu_ref,
                      wd_ref, fpost_ref, out_ref):
    tm = h_ref.shape[1]
    groups = [slice(r, r + MERGE_ROWS) for r in range(0, tm, MERGE_ROWS)]
    h1 = []
    for rows in groups:
        yattn = _dot(ot_ref[0, :, rows], wao_ref[...], _TN)
        merged = g0_ref[0, rows, :].astype(F32) * yattn + part_ref[0, rows, :].astype(F32)
        h1.append(h_ref[0, rows, :] + _rms(_dot(merged.astype(BF16), wmo_ref[...]), npost_ref[...]))
    act = []
    for g in range(len(groups)):
        hn = _rms(h1[g], fpre_ref[...]).astype(BF16)
        gate = _dot(hn, wg_ref[...])
        act.append((gate * jax.nn.sigmoid(gate) * _dot(hn, wu_ref[...])).astype(BF16))
    for g, rows in enumerate(groups):
        out_ref[0, rows, :] = h1[g] + _rms(_dot(act[g], wd_ref[...]), fpost_ref[...])


def _merge_ffn(h, o_t, g0, part, p, l, *, tm):
    b, s, d = h.shape
    row_tile = lambda w: pl.BlockSpec((1, tm, w), lambda bi, i: (bi, i, 0))
    assert s % tm == 0 and tm % MERGE_ROWS == 0
    consts = [p["w_attn_o"], p["w_mix_o"], p["npost"], p["fpre"], p["w_gate"], p["w_up"], p["w_down"], p["fpost"]]
    return pl.pallas_call(
        _merge_ffn_kernel,
        grid=(b, s // tm),
        in_specs=[row_tile(d), pl.BlockSpec((1, N_HEADS * V_DIM, tm), lambda bi, i: (bi, 0, i)), row_tile(d),
                  row_tile(d)] + [_layer_spec(c, l) for c in consts],
        out_specs=row_tile(d),
        out_shape=jax.ShapeDtypeStruct((b, s, d), F32),
        compiler_params=pltpu.CompilerParams(
            dimension_semantics=("arbitrary", "arbitrary"), vmem_limit_bytes=VMEM_LIMIT_BYTES),
        name="merge_ffn",
    )(h, o_t, g0, part, *consts)


def _stacked_params(mix_norm_pre, w_in, q_norm, w_uq, kv_norm, w_uk, w_uv, w_attn_o, conv_w, conv_b, conv_ln_g,
                    conv_ln_b, w_conv_o, pool_w, pool_scale, w_pool_o, w_mix_o, mix_norm_post, ffn_norm_pre, w_gate,
                    w_up, w_down, ffn_norm_post):
    depth = w_in.shape[0]
    row = lambda v: v.reshape(depth, 1, -1)
    w_in = lax.optimization_barrier(w_in.astype(BF16))
    w_kr = jnp.pad(w_in[:, :, LAT_COLS:WIDE_START], ((0, 0), (0, 0), (0, HEAD_PAD - ROPE)))
    assert w_in.shape[2] - WIDE_START == SEG_GATES[1]

    def pad_heads(w, dh, dh_pad):
        w = jnp.pad(w.reshape(depth, w.shape[1], N_HEADS, dh), ((0, 0), (0, 0), (0, 0), (0, dh_pad - dh)))
        return w.reshape(depth, w.shape[1], N_HEADS * dh_pad)

    transpose = lambda w: jnp.swapaxes(w, 1, 2)
    pool_bd = jnp.zeros((depth, POOL_CH, POOL_CH), F32)
    for gi in range(len(POOL_WINDOWS)):
        sl = slice(gi * POOL_GD, (gi + 1) * POOL_GD)
        pool_bd = pool_bd.at[:, sl, sl].set(pool_w[:, gi])
    return dict(
        npre=row(mix_norm_pre), w_lat=w_in[:, :, :LAT_COLS], w_kr=w_kr,
        w_wide=w_in[:, :, WIDE_START:], q_norm=row(q_norm),
        w_uq_t=transpose(pad_heads(w_uq, QK_DIM, HEAD_PAD)).astype(BF16), kv_norm=row(kv_norm),
        w_uk=pad_heads(w_uk, NOPE, HEAD_PAD).astype(BF16),
        w_uv_t=transpose(pad_heads(w_uv, V_DIM, V_ROWS)).astype(BF16),
        conv_w=conv_w, conv_b=row(conv_b), ln_g=row(conv_ln_g), ln_b=row(conv_ln_b),
        w_conv_o=w_conv_o.astype(BF16), pool_bd=pool_bd.astype(BF16), pool_scale=row(pool_scale),
        w_pool_o=w_pool_o.astype(BF16),
        w_attn_o=w_attn_o.astype(BF16), w_mix_o=w_mix_o.astype(BF16), npost=row(mix_norm_post),
        fpre=row(ffn_norm_pre), w_gate=w_gate.astype(BF16), w_up=w_up.astype(BF16),
        w_down=w_down.astype(BF16), fpost=row(ffn_norm_post),
    )


def kernel(x, positions, mix_norm_pre, w_in, q_norm, w_uq, kv_norm, w_uk, w_uv, w_attn_o, conv_w, conv_b, conv_ln_g, conv_ln_b, w_conv_o, pool_w, pool_scale, w_pool_o, w_mix_o, mix_norm_post, ffn_norm_pre, w_gate, w_up, w_down, ffn_norm_post):
    p = _stacked_params(mix_norm_pre, w_in, q_norm, w_uq, kv_norm, w_uk, w_uv, w_attn_o, conv_w, conv_b, conv_ln_g,
                        conv_ln_b, w_conv_o, pool_w, pool_scale, w_pool_o, w_mix_o, mix_norm_post, ffn_norm_pre,
                        w_gate, w_up, w_down, ffn_norm_post)
    assert x.dtype == F32 and x.shape[2] == D_MODEL and positions.shape == x.shape[:2]
    s = x.shape[1]
    tm = min(ROW_TILE, s)
    cos_t, sin_t = _rope_tables(positions)
    h = x
    for l in range(w_in.shape[0]):
        q_t, k, v_t, g0, part = _mixer_in(h, cos_t, sin_t, p, l, tm=tm)
        o_t = _attention(q_t, k, v_t, tq=min(ATTN_TQ, s))
        h = _merge_ffn(h, o_t, g0, part, p, l, tm=tm)
    return h
```

```python
import functools
import math

import jax
import jax.numpy as jnp
from jax import lax
from jax.experimental import pallas as pl
from jax.experimental.pallas import tpu as pltpu

D_MODEL = 1024
N_HEADS = 8
NOPE = 64
ROPE = 32
ROPE_HALF = ROPE // 2
V_DIM = 64
Q_RANK = 384
KV_RANK = 256
ROPE_THETA = 10000.0
CONV_CH = 512
CONV_W = 31
POOL_WINDOWS = (2, 4, 8, 16)
POOL_CH = 512
POOL_GD = POOL_CH // len(POOL_WINDOWS)
N_BRANCHES = 3
D_FF = 2816
EPS = 1e-6

LANES = 128
SUBLANES = 8
BF16_ROWS = 16
MXU_DIM = 256
VMEM_LIMIT_BYTES = 56 * 1024 * 1024

HEAD_PAD = LANES
QK_DIM = NOPE + ROPE
V_ROWS = V_DIM + BF16_ROWS
Q_SCALE = (1.0 / math.sqrt(QK_DIM)) * math.log2(math.e)

LAT_COLS = Q_RANK + KV_RANK
WIDE_START = LAT_COLS + ROPE
SEG_Q = (0, Q_RANK)
SEG_KV = (Q_RANK, LAT_COLS)
SEG_CA = (0, CONV_CH)
SEG_CG = (CONV_CH, 2 * CONV_CH)
SEG_POOL = (2 * CONV_CH, 2 * CONV_CH + POOL_CH)
SEG_GATES = (SEG_POOL[1], SEG_POOL[1] + N_BRANCHES * D_MODEL)

CONV_HALO = 32
POOL_HALO = 16
CONV_ROWS = 32

ROW_TILE = 512
MERGE_ROWS = 256
ATTN_TQ = 512

F32 = jnp.float32
BF16 = jnp.bfloat16

_NN = (((1,), (0,)), ((), ()))
_NT = (((1,), (1,)), ((), ()))
_TN = (((0,), (0,)), ((), ()))


def _rms(x, g):
    return x * lax.rsqrt(jnp.mean(x * x, axis=-1, keepdims=True) + EPS) * g


def _dot(a, b, dims=_NN):
    return lax.dot_general(a, b, dims, preferred_element_type=F32)


def _rope_rows(x1, x2, cos, sin):
    return x1 * cos - x2 * sin, x2 * cos + x1 * sin


def _rope_table_kernel(pos_ref, invf_ref, cos_ref, sin_ref):
    ang = invf_ref[...] * pos_ref[0].astype(F32)
    cos_ref[0] = jnp.cos(ang)
    sin_ref[0] = jnp.sin(ang)


def _rope_tables(positions):
    b, s = positions.shape
    inv_freq = ROPE_THETA ** (-jnp.arange(0, ROPE, 2, dtype=F32) / ROPE)
    out = jax.ShapeDtypeStruct((b, ROPE_HALF, s), F32)
    return pl.pallas_call(
        _rope_table_kernel,
        grid=(b,),
        in_specs=[
            pl.BlockSpec((1, 1, s), lambda i: (i, 0, 0)),
            pl.BlockSpec((ROPE_HALF, 1), lambda i: (0, 0)),
        ],
        out_specs=[
            pl.BlockSpec((1, ROPE_HALF, s), lambda i: (i, 0, 0)),
            pl.BlockSpec((1, ROPE_HALF, s), lambda i: (i, 0, 0)),
        ],
        out_shape=[out, out],
        name="rope_tables",
    )(positions.reshape(b, 1, s), inv_freq.reshape(ROPE_HALF, 1))


def _mixer_in_kernel(x_ref, cos_ref, sin_ref, npre_ref, wlat_ref, wkr_ref, wwide_ref, qn_ref, wuqt_ref, kvn_ref,
                     wuk_ref, wuvt_ref, cw_ref, cb_ref, lng_ref, lnb_ref, wco_ref, pbd_ref, ps_ref, wpo_ref,
                     qt_ref, k_ref, vt_ref, g0_ref, part_ref,
                     cext_ref, cshift_ref, cwb_ref, conv_ref, pext_ref, gates_ref, *, tm):
    i = pl.program_id(1)
    hn = [_rms(x_ref[0, r:r + MERGE_ROWS, :], npre_ref[...]).astype(BF16) for r in range(0, tm, MERGE_ROWS)]

    def proj(w_ref, lo, hi):
        return jnp.concatenate([_dot(g, w_ref[:, lo:hi]) for g in hn], axis=0)

    def rows_dot(a, w):
        return jnp.concatenate([_dot(a[r:r + MERGE_ROWS], w) for r in range(0, tm, MERGE_ROWS)], axis=0)

    @pl.when(i == 0)
    def _():
        cext_ref[0:CONV_HALO, :] = jnp.zeros((CONV_HALO, CONV_CH), F32)
        pext_ref[0:POOL_HALO, :] = jnp.zeros((POOL_HALO, POOL_CH), F32)
        for j in range(CONV_W):
            cwb_ref[j] = jnp.broadcast_to(cw_ref[j:j + 1, :], (SUBLANES, CONV_CH))

    cext_ref[CONV_HALO:CONV_HALO + tm, :] = proj(wwide_ref, *SEG_CA) * jax.nn.sigmoid(proj(wwide_ref, *SEG_CG))
    ext_rows = tm + CONV_HALO
    ext = cext_ref[...]
    for r in range(1, SUBLANES):
        cshift_ref[r - 1] = pltpu.roll(ext, ext_rows - r, 0)[0:ext_rows - SUBLANES]
    first_shift = CONV_HALO - (CONV_W - 1)
    n_conv_steps = tm // CONV_ROWS
    gate_slabs = (SEG_GATES[1] - SEG_GATES[0]) // MXU_DIM
    slab = 0
    for step in range(n_conv_steps):
        r0 = step * CONV_ROWS
        accs = [None] * (CONV_ROWS // SUBLANES)
        for j in range(CONV_W):
            s = first_shift + j
            r, base = s % SUBLANES, r0 + s - s % SUBLANES
            w_tile = cwb_ref[j]
            for g in range(len(accs)):
                rows = slice(base + g * SUBLANES, base + (g + 1) * SUBLANES)
                src = cext_ref[rows, :] if r == 0 else cshift_ref[r - 1, rows, :]
                accs[g] = src * w_tile if accs[g] is None else accs[g] + src * w_tile
        conv_ref[r0:r0 + CONV_ROWS, :] = jnp.concatenate(accs, axis=0)
        slab_end = (step + 1) * gate_slabs // n_conv_steps
        while slab < slab_end:
            lo = slab * MXU_DIM
            gates_ref[:, lo:lo + MXU_DIM] = jax.nn.sigmoid(
                proj(wwide_ref, SEG_GATES[0] + lo, SEG_GATES[0] + lo + MXU_DIM))
            slab += 1
    cext_ref[0:CONV_HALO, :] = cext_ref[tm:tm + CONV_HALO, :]

    cos = cos_ref[0]
    sin = sin_ref[0]

    cq = _rms(proj(wlat_ref, *SEG_Q), qn_ref[...]).astype(BF16)
    qt = _dot(wuqt_ref[...], cq, _NT)
    for h in range(N_HEADS):
        base = h * HEAD_PAD
        r1, r2 = _rope_rows(qt[base + NOPE:base + NOPE + ROPE_HALF], qt[base + NOPE + ROPE_HALF:base + QK_DIM],
                            cos, sin)
        blk = jnp.concatenate([qt[base:base + NOPE], r1, r2, qt[base + QK_DIM:base + HEAD_PAD]], axis=0)
        qt_ref[0, base:base + HEAD_PAD, :] = (blk * Q_SCALE).astype(BF16)

    ckv = _rms(proj(wlat_ref, *SEG_KV), kvn_ref[...]).astype(BF16)
    kpad = rows_dot(ckv, wuk_ref[...])
    row_id = lax.broadcasted_iota(jnp.int32, (N_HEADS * V_ROWS, 1), 0)
    ones_row = sum((row_id == h * V_ROWS + V_DIM).astype(F32) for h in range(N_HEADS))
    vt_ref[0] = (_dot(wuvt_ref[...], ckv, _NT) + ones_row).astype(BF16)
    krt = proj(wkr_ref, 0, HEAD_PAD).T
    r1, r2 = _rope_rows(krt[0:ROPE_HALF], krt[ROPE_HALF:ROPE], cos, sin)
    kr = jnp.concatenate([jnp.zeros((NOPE, tm), F32), r1, r2, jnp.zeros((HEAD_PAD - QK_DIM, tm), F32)], axis=0).T
    for h in range(N_HEADS):
        base = h * HEAD_PAD
        k_ref[0, :, base:base + HEAD_PAD] = (kpad[:, base:base + HEAD_PAD] + kr).astype(BF16)

    hc = conv_ref[...] + cb_ref[...]
    xc = hc - jnp.mean(hc, axis=-1, keepdims=True)
    ln = xc * lax.rsqrt(jnp.mean(xc * xc, axis=-1, keepdims=True) + EPS) * lng_ref[...] + lnb_ref[...]
    yconv = rows_dot((ln * jax.nn.sigmoid(ln)).astype(BF16), wco_ref[...])

    u = proj(wwide_ref, *SEG_POOL)
    pext_ref[POOL_HALO:POOL_HALO + tm, :] = u
    t1 = i * tm + 1 + lax.broadcasted_iota(jnp.int32, (tm, 1), 0)
    pool_rows = tm + POOL_HALO
    run = pext_ref[...]
    diffs = []
    for gi, w in enumerate(POOL_WINDOWS):
        run = run + pltpu.roll(run, w // 2, 0)
        tot = run[POOL_HALO:pool_rows, 0:POOL_GD]
        cnt = jnp.minimum(t1, w).astype(F32)
        diffs.append(tot / cnt - u[:, gi * POOL_GD:(gi + 1) * POOL_GD])
        run = run[:, POOL_GD:]
    pext_ref[0:POOL_HALO, :] = pext_ref[tm:tm + POOL_HALO, :]
    dmix = jnp.concatenate(diffs, axis=1).astype(BF16)
    mixed = rows_dot(dmix, pbd_ref[...]) * ps_ref[...]
    ypool = rows_dot(mixed.astype(BF16), wpo_ref[...])

    part = gates_ref[:, D_MODEL:2 * D_MODEL] * yconv + gates_ref[:, 2 * D_MODEL:3 * D_MODEL] * ypool
    part_ref[0] = part.astype(BF16)
    g0_ref[0] = gates_ref[:, 0:D_MODEL].astype(BF16)


def _layer_spec(stacked, l):
    zeros = (0,) * (stacked.ndim - 1)
    return pl.BlockSpec((None,) + stacked.shape[1:], lambda b, i: (l,) + zeros, pipeline_mode=pl.Buffered(1))


def _mixer_in(x, cos_t, sin_t, p, l, *, tm):
    b, s, d = x.shape
    nt = s // tm
    row_tile = lambda w: pl.BlockSpec((1, tm, w), lambda bi, i: (bi, i, 0))
    col_tile = lambda r: pl.BlockSpec((1, r, tm), lambda bi, i: (bi, 0, i))
    assert s % tm == 0 and tm % MERGE_ROWS == 0 and tm % CONV_ROWS == 0
    consts = [p[name] for name in (
        "npre", "w_lat", "w_kr", "w_wide", "q_norm", "w_uq_t", "kv_norm", "w_uk", "w_uv_t", "conv_w", "conv_b",
        "ln_g", "ln_b", "w_conv_o", "pool_bd", "pool_scale", "w_pool_o")]
    return pl.pallas_call(
        functools.partial(_mixer_in_kernel, tm=tm),
        grid=(b, nt),
        in_specs=[row_tile(d), col_tile(ROPE_HALF), col_tile(ROPE_HALF)] + [_layer_spec(c, l) for c in consts],
        out_specs=[col_tile(N_HEADS * HEAD_PAD), row_tile(N_HEADS * HEAD_PAD), col_tile(N_HEADS * V_ROWS),
                   row_tile(d), row_tile(d)],
        out_shape=[
            jax.ShapeDtypeStruct((b, N_HEADS * HEAD_PAD, s), BF16),
            jax.ShapeDtypeStruct((b, s, N_HEADS * HEAD_PAD), BF16),
            jax.ShapeDtypeStruct((b, N_HEADS * V_ROWS, s), BF16),
            jax.ShapeDtypeStruct((b, s, d), BF16),
            jax.ShapeDtypeStruct((b, s, d), BF16),
        ],
        scratch_shapes=[
            pltpu.VMEM((tm + CONV_HALO, CONV_CH), F32),
            pltpu.VMEM((SUBLANES - 1, tm + CONV_HALO - SUBLANES, CONV_CH), F32),
            pltpu.VMEM((CONV_W, SUBLANES, CONV_CH), F32),
            pltpu.VMEM((tm, CONV_CH), F32),
            pltpu.VMEM((tm + POOL_HALO, POOL_CH), F32),
            pltpu.VMEM((tm, N_BRANCHES * D_MODEL), F32),
        ],
        compiler_params=pltpu.CompilerParams(
            dimension_semantics=("arbitrary", "arbitrary"), vmem_limit_bytes=VMEM_LIMIT_BYTES),
        name="mixer_in",
    )(x, cos_t, sin_t, *consts)


def _attn_kernel(qt_ref, k_ref, vt_ref, ot_ref, s0_ref, s1_ref, m_ref, acc_ref, *, tq):
    i = pl.program_id(1)
    half = tq // 2
    m_ref[...] = jnp.full(m_ref.shape, -jnp.inf, F32)
    acc_ref[...] = jnp.zeros(acc_ref.shape, F32)

    def head(h):
        return slice(h * HEAD_PAD, (h + 1) * HEAD_PAD)

    def scores_into(s_ref, c):
        k0 = pl.multiple_of(c * tq, tq)
        for h in range(N_HEADS):
            s_ref[h] = _dot(k_ref[0, pl.ds(k0, tq), head(h)], qt_ref[0, head(h), :])

    def update(s_ref, c, k_lo, n_keys, q_lo, masked):
        k0 = pl.multiple_of(c * tq + k_lo, half)
        for h in range(N_HEADS):
            s = s_ref[h, k_lo:k_lo + n_keys, q_lo:tq]
            if masked:
                key = lax.broadcasted_iota(jnp.int32, (n_keys, 1), 0) + (k_lo - q_lo)
                qry = lax.broadcasted_iota(jnp.int32, (1, tq - q_lo), 1)
                s = jnp.where(key <= qry, s, -jnp.inf)
            m = m_ref[h, :, q_lo:tq]
            m_new = jnp.maximum(m, jnp.max(s, axis=0, keepdims=True))
            alpha = jnp.exp2(m - m_new)
            prob = jnp.exp2(s - m_new).astype(BF16)
            m_ref[h, :, q_lo:tq] = m_new
            vc = vt_ref[0, h * V_ROWS:(h + 1) * V_ROWS, pl.ds(k0, n_keys)]
            acc_ref[h, :, q_lo:tq] = alpha * acc_ref[h, :, q_lo:tq] + _dot(vc, prob)

    def diagonal(s_ref):
        update(s_ref, i, 0, half, 0, True)
        update(s_ref, i, half, half, half, True)

    def pair(p, carry):
        scores_into(s1_ref, 2 * p + 1)
        update(s0_ref, 2 * p, 0, tq, 0, False)
        scores_into(s0_ref, 2 * p + 2)
        update(s1_ref, 2 * p + 1, 0, tq, 0, False)
        return carry

    def tail_even():
        diagonal(s0_ref)

    def tail_odd():
        scores_into(s1_ref, i)
        update(s0_ref, i - 1, 0, tq, 0, False)
        diagonal(s1_ref)

    @pl.when(i == 0)
    def _():
        scores_into(s0_ref, 0)
        tail_even()

    @pl.when(i == 1)
    def _():
        scores_into(s0_ref, 0)
        tail_odd()

    @pl.when(i >= 2)
    def _():
        scores_into(s0_ref, 0)
        pair(0, 0)

    lax.fori_loop(1, i // 2, pair, 0)

    @pl.when(jnp.logical_and(i >= 2, i % 2 == 0))
    def _():
        tail_even()

    @pl.when(jnp.logical_and(i >= 2, i % 2 == 1))
    def _():
        tail_odd()

    for h in range(N_HEADS):
        ot_ref[0, h * V_DIM:(h + 1) * V_DIM, :] = (acc_ref[h, 0:V_DIM, :] / acc_ref[h, V_DIM:V_DIM + 1, :]).astype(BF16)


def _attention(q_t, k, v_t, *, tq):
    b, _, s = q_t.shape
    tk = tq
    assert s % tq == 0
    return pl.pallas_call(
        functools.partial(_attn_kernel, tq=tq),
        grid=(b, s // tq),
        in_specs=[
            pl.BlockSpec((1, N_HEADS * HEAD_PAD, tq), lambda bi, i: (bi, 0, i)),
            pl.BlockSpec((1, s, N_HEADS * HEAD_PAD), lambda bi, i: (bi, 0, 0)),
            pl.BlockSpec((1, N_HEADS * V_ROWS, s), lambda bi, i: (bi, 0, 0)),
        ],
        out_specs=pl.BlockSpec((1, N_HEADS * V_DIM, tq), lambda bi, i: (bi, 0, i)),
        out_shape=jax.ShapeDtypeStruct((b, N_HEADS * V_DIM, s), BF16),
        scratch_shapes=[
            pltpu.VMEM((N_HEADS, tk, tq), F32),
            pltpu.VMEM((N_HEADS, tk, tq), F32),
            pltpu.VMEM((N_HEADS, 1, tq), F32),
            pltpu.VMEM((N_HEADS, V_ROWS, tq), F32),
        ],
        compiler_params=pltpu.CompilerParams(
            dimension_semantics=("arbitrary", "arbitrary"), vmem_limit_bytes=VMEM_LIMIT_BYTES),
        name="attention",
    )(q_t, k, v_t)


def _merge_ffn_kernel(h_ref, ot_ref, g0_ref, part_ref, wao_ref, wmo_ref, npost_ref, fpre_ref, wg_ref, wu_ref,
                      wd_ref, fpost_ref, out_ref):
    tm = h_ref.shape[1]
    groups = [slice(r, r + MERGE_ROWS) for r in range(0, tm, MERGE_ROWS)]
    h1 = []
    for rows in groups:
        yattn = _dot(ot_ref[0, :, rows], wao_ref[...], _TN)
        merged = g0_ref[0, rows, :].astype(F32) * yattn + part_ref[0, rows, :].astype(F32)
        h1.append(h_ref[0, rows, :] + _rms(_dot(merged.astype(BF16), wmo_ref[...]), npost_ref[...]))
    act = []
    for g in range(len(groups)):
        hn = _rms(h1[g], fpre_ref[...]).astype(BF16)
        gate = _dot(hn, wg_ref[...])
        act.append((gate * jax.nn.sigmoid(gate) * _dot(hn, wu_ref[...])).astype(BF16))
    for g, rows in enumerate(groups):
        out_ref[0, rows, :] = h1[g] + _rms(_dot(act[g], wd_ref[...]), fpost_ref[...])


def _merge_ffn(h, o_t, g0, part, p, l, *, tm):
    b, s, d = h.shape
    row_tile = lambda w: pl.BlockSpec((1, tm, w), lambda bi, i: (bi, i, 0))
    assert s % tm == 0 and tm % MERGE_ROWS == 0
    consts = [p["w_attn_o"], p["w_mix_o"], p["npost"], p["fpre"], p["w_gate"], p["w_up"], p["w_down"], p["fpost"]]
    return pl.pallas_call(
        _merge_ffn_kernel,
        grid=(b, s // tm),
        in_specs=[row_tile(d), pl.BlockSpec((1, N_HEADS * V_DIM, tm), lambda bi, i: (bi, 0, i)), row_tile(d),
                  row_tile(d)] + [_layer_spec(c, l) for c in consts],
        out_specs=row_tile(d),
        out_shape=jax.ShapeDtypeStruct((b, s, d), F32),
        compiler_params=pltpu.CompilerParams(
            dimension_semantics=("arbitrary", "arbitrary"), vmem_limit_bytes=VMEM_LIMIT_BYTES),
        name="merge_ffn",
    )(h, o_t, g0, part, *consts)


def _stacked_params(mix_norm_pre, w_in, q_norm, w_uq, kv_norm, w_uk, w_uv, w_attn_o, conv_w, conv_b, conv_ln_g,
                    conv_ln_b, w_conv_o, pool_w, pool_scale, w_pool_o, w_mix_o, mix_norm_post, ffn_norm_pre, w_gate,
                    w_up, w_down, ffn_norm_post):
    depth = w_in.shape[0]
    row = lambda v: v.reshape(depth, 1, -1)
    w_in = w_in.astype(BF16)
    w_kr = jnp.pad(w_in[:, :, LAT_COLS:WIDE_START], ((0, 0), (0, 0), (0, HEAD_PAD - ROPE)))
    assert w_in.shape[2] - WIDE_START == SEG_GATES[1]

    def pad_heads(w, dh, dh_pad):
        w = jnp.pad(w.reshape(depth, w.shape[1], N_HEADS, dh), ((0, 0), (0, 0), (0, 0), (0, dh_pad - dh)))
        return w.reshape(depth, w.shape[1], N_HEADS * dh_pad)

    transpose = lambda w: jnp.swapaxes(w, 1, 2)
    pool_bd = jnp.zeros((depth, POOL_CH, POOL_CH), F32)
    for gi in range(len(POOL_WINDOWS)):
        sl = slice(gi * POOL_GD, (gi + 1) * POOL_GD)
        pool_bd = pool_bd.at[:, sl, sl].set(pool_w[:, gi])
    return dict(
        npre=row(mix_norm_pre), w_lat=w_in[:, :, :LAT_COLS], w_kr=w_kr,
        w_wide=w_in[:, :, WIDE_START:], q_norm=row(q_norm),
        w_uq_t=transpose(pad_heads(w_uq, QK_DIM, HEAD_PAD)).astype(BF16), kv_norm=row(kv_norm),
        w_uk=pad_heads(w_uk, NOPE, HEAD_PAD).astype(BF16),
        w_uv_t=transpose(pad_heads(w_uv, V_DIM, V_ROWS)).astype(BF16),
        conv_w=conv_w, conv_b=row(conv_b), ln_g=row(conv_ln_g), ln_b=row(conv_ln_b),
        w_conv_o=w_conv_o.astype(BF16), pool_bd=pool_bd.astype(BF16), pool_scale=row(pool_scale),
        w_pool_o=w_pool_o.astype(BF16),
        w_attn_o=w_attn_o.astype(BF16), w_mix_o=w_mix_o.astype(BF16), npost=row(mix_norm_post),
        fpre=row(ffn_norm_pre), w_gate=w_gate.astype(BF16), w_up=w_up.astype(BF16),
        w_down=w_down.astype(BF16), fpost=row(ffn_norm_post),
    )


def kernel(x, positions, mix_norm_pre, w_in, q_norm, w_uq, kv_norm, w_uk, w_uv, w_attn_o, conv_w, conv_b, conv_ln_g, conv_ln_b, w_conv_o, pool_w, pool_scale, w_pool_o, w_mix_o, mix_norm_post, ffn_norm_pre, w_gate, w_up, w_down, ffn_norm_post):
    p = _stacked_params(mix_norm_pre, w_in, q_norm, w_uq, kv_norm, w_uk, w_uv, w_attn_o, conv_w, conv_b, conv_ln_g,
                        conv_ln_b, w_conv_o, pool_w, pool_scale, w_pool_o, w_mix_o, mix_norm_post, ffn_norm_pre,
                        w_gate, w_up, w_down, ffn_norm_post)
    assert x.dtype == F32 and x.shape[2] == D_MODEL and positions.shape == x.shape[:2]
    s = x.shape[1]
    tm = min(ROW_TILE, s)
    cos_t, sin_t = _rope_tables(positions)
    h = x
    for l in range(w_in.shape[0]):
        q_t, k, v_t, g0, part = _mixer_in(h, cos_t, sin_t, p, l, tm=tm)
        o_t = _attention(q_t, k, v_t, tq=min(ATTN_TQ, s))
        h = _merge_ffn(h, o_t, g0, part, p, l, tm=tm)
    return h
```

```python
import functools
import math

import jax
import jax.numpy as jnp
from jax import lax
from jax.experimental import pallas as pl
from jax.experimental.pallas import tpu as pltpu

D_MODEL = 1024
N_HEADS = 8
NOPE = 64
ROPE = 32
ROPE_HALF = ROPE // 2
V_DIM = 64
Q_RANK = 384
KV_RANK = 256
ROPE_THETA = 10000.0
CONV_CH = 512
CONV_W = 31
POOL_WINDOWS = (2, 4, 8, 16)
POOL_CH = 512
POOL_GD = POOL_CH // len(POOL_WINDOWS)
N_BRANCHES = 3
D_FF = 2816
EPS = 1e-6

LANES = 128
SUBLANES = 8
BF16_ROWS = 16
MXU_DIM = 256
VMEM_LIMIT_BYTES = 56 * 1024 * 1024

HEAD_PAD = LANES
QK_DIM = NOPE + ROPE
V_ROWS = V_DIM + BF16_ROWS
Q_SCALE = (1.0 / math.sqrt(QK_DIM)) * math.log2(math.e)

LAT_COLS = Q_RANK + KV_RANK
WIDE_START = LAT_COLS + ROPE
SEG_Q = (0, Q_RANK)
SEG_KV = (Q_RANK, LAT_COLS)
SEG_CA = (0, CONV_CH)
SEG_CG = (CONV_CH, 2 * CONV_CH)
SEG_POOL = (2 * CONV_CH, 2 * CONV_CH + POOL_CH)
SEG_GATES = (SEG_POOL[1], SEG_POOL[1] + N_BRANCHES * D_MODEL)

CONV_HALO = 32
POOL_HALO = 16
CONV_ROWS = 32

ROW_TILE = 512
MERGE_ROWS = 256
ATTN_TQ = 512

F32 = jnp.float32
BF16 = jnp.bfloat16

_NN = (((1,), (0,)), ((), ()))
_NT = (((1,), (1,)), ((), ()))
_TN = (((0,), (0,)), ((), ()))


def _rms(x, g):
    return x * lax.rsqrt(jnp.mean(x * x, axis=-1, keepdims=True) + EPS) * g


def _dot(a, b, dims=_NN):
    return lax.dot_general(a, b, dims, preferred_element_type=F32)


def _rope_rows(x1, x2, cos, sin):
    return x1 * cos - x2 * sin, x2 * cos + x1 * sin


def _rope_table_kernel(pos_ref, invf_ref, cos_ref, sin_ref):
    ang = invf_ref[...] * pos_ref[0].astype(F32)
    cos_ref[0] = jnp.cos(ang)
    sin_ref[0] = jnp.sin(ang)


def _rope_tables(positions):
    b, s = positions.shape
    inv_freq = ROPE_THETA ** (-jnp.arange(0, ROPE, 2, dtype=F32) / ROPE)
    out = jax.ShapeDtypeStruct((b, ROPE_HALF, s), F32)
    return pl.pallas_call(
        _rope_table_kernel,
        grid=(b,),
        in_specs=[
            pl.BlockSpec((1, 1, s), lambda i: (i, 0, 0)),
            pl.BlockSpec((ROPE_HALF, 1), lambda i: (0, 0)),
        ],
        out_specs=[
            pl.BlockSpec((1, ROPE_HALF, s), lambda i: (i, 0, 0)),
            pl.BlockSpec((1, ROPE_HALF, s), lambda i: (i, 0, 0)),
        ],
        out_shape=[out, out],
        name="rope_tables",
    )(positions.reshape(b, 1, s), inv_freq.reshape(ROPE_HALF, 1))


def _mixer_in_kernel(x_ref, cos_ref, sin_ref, npre_ref, wlat_ref, wkr_ref, wwide_ref, qn_ref, wuqt_ref, kvn_ref,
                     wuk_ref, wuvt_ref, cw_ref, cb_ref, lng_ref, lnb_ref, wco_ref, pbd_ref, ps_ref, wpo_ref,
                     qt_ref, k_ref, vt_ref, g0_ref, part_ref,
                     cext_ref, cshift_ref, cwb_ref, conv_ref, pext_ref, gates_ref, *, tm):
    i = pl.program_id(1)
    hn = [_rms(x_ref[0, r:r + MERGE_ROWS, :], npre_ref[...]).astype(BF16) for r in range(0, tm, MERGE_ROWS)]

    def proj(w_ref, lo, hi):
        return jnp.concatenate([_dot(g, w_ref[:, lo:hi]) for g in hn], axis=0)

    def rows_dot(a, w):
        return jnp.concatenate([_dot(a[r:r + MERGE_ROWS], w) for r in range(0, tm, MERGE_ROWS)], axis=0)

    @pl.when(i == 0)
    def _():
        cext_ref[0:CONV_HALO, :] = jnp.zeros((CONV_HALO, CONV_CH), F32)
        pext_ref[0:POOL_HALO, :] = jnp.zeros((POOL_HALO, POOL_CH), F32)
        for j in range(CONV_W):
            cwb_ref[j] = jnp.broadcast_to(cw_ref[j:j + 1, :], (SUBLANES, CONV_CH))

    cext_ref[CONV_HALO:CONV_HALO + tm, :] = proj(wwide_ref, *SEG_CA) * jax.nn.sigmoid(proj(wwide_ref, *SEG_CG))
    ext_rows = tm + CONV_HALO
    ext = cext_ref[...]
    for r in range(1, SUBLANES):
        cshift_ref[r - 1] = pltpu.roll(ext, ext_rows - r, 0)[0:ext_rows - SUBLANES]
    first_shift = CONV_HALO - (CONV_W - 1)
    n_conv_steps = tm // CONV_ROWS
    gate_slabs = (SEG_GATES[1] - SEG_GATES[0]) // MXU_DIM
    slab = 0
    for step in range(n_conv_steps):
        r0 = step * CONV_ROWS
        accs = [None] * (CONV_ROWS // SUBLANES)
        for j in range(CONV_W):
            s = first_shift + j
            r, base = s % SUBLANES, r0 + s - s % SUBLANES
            w_tile = cwb_ref[j]
            for g in range(len(accs)):
                rows = slice(base + g * SUBLANES, base + (g + 1) * SUBLANES)
                src = cext_ref[rows, :] if r == 0 else cshift_ref[r - 1, rows, :]
                accs[g] = src * w_tile if accs[g] is None else accs[g] + src * w_tile
        conv_ref[r0:r0 + CONV_ROWS, :] = jnp.concatenate(accs, axis=0)
        slab_end = (step + 1) * gate_slabs // n_conv_steps
        while slab < slab_end:
            lo = slab * MXU_DIM
            gates_ref[:, lo:lo + MXU_DIM] = jax.nn.sigmoid(
                proj(wwide_ref, SEG_GATES[0] + lo, SEG_GATES[0] + lo + MXU_DIM))
            slab += 1
    cext_ref[0:CONV_HALO, :] = cext_ref[tm:tm + CONV_HALO, :]

    cos = cos_ref[0]
    sin = sin_ref[0]

    cq = _rms(proj(wlat_ref, *SEG_Q), qn_ref[...]).astype(BF16)
    qt = _dot(wuqt_ref[...], cq, _NT)
    for h in range(N_HEADS):
        base = h * HEAD_PAD
        r1, r2 = _rope_rows(qt[base + NOPE:base + NOPE + ROPE_HALF], qt[base + NOPE + ROPE_HALF:base + QK_DIM],
                            cos, sin)
        blk = jnp.concatenate([qt[base:base + NOPE], r1, r2, qt[base + QK_DIM:base + HEAD_PAD]], axis=0)
        qt_ref[0, base:base + HEAD_PAD, :] = (blk * Q_SCALE).astype(BF16)

    ckv = _rms(proj(wlat_ref, *SEG_KV), kvn_ref[...]).astype(BF16)
    kpad = rows_dot(ckv, wuk_ref[...])
    row_id = lax.broadcasted_iota(jnp.int32, (N_HEADS * V_ROWS, 1), 0)
    ones_row = sum((row_id == h * V_ROWS + V_DIM).astype(F32) for h in range(N_HEADS))
    vt_ref[0] = (_dot(wuvt_ref[...], ckv, _NT) + ones_row).astype(BF16)
    krt = proj(wkr_ref, 0, HEAD_PAD).T
    r1, r2 = _rope_rows(krt[0:ROPE_HALF], krt[ROPE_HALF:ROPE], cos, sin)
    kr = jnp.concatenate([jnp.zeros((NOPE, tm), F32), r1, r2, jnp.zeros((HEAD_PAD - QK_DIM, tm), F32)], axis=0).T
    for h in range(N_HEADS):
        base = h * HEAD_PAD
        k_ref[0, :, base:base + HEAD_PAD] = (kpad[:, base:base + HEAD_PAD] + kr).astype(BF16)

    hc = conv_ref[...] + cb_ref[...]
    xc = hc - jnp.mean(hc, axis=-1, keepdims=True)
    ln = xc * lax.rsqrt(jnp.mean(xc * xc, axis=-1, keepdims=True) + EPS) * lng_ref[...] + lnb_ref[...]
    yconv = rows_dot((ln * jax.nn.sigmoid(ln)).astype(BF16), wco_ref[...])

    u = proj(wwide_ref, *SEG_POOL)
    pext_ref[POOL_HALO:POOL_HALO + tm, :] = u
    t1 = i * tm + 1 + lax.broadcasted_iota(jnp.int32, (tm, 1), 0)
    pool_rows = tm + POOL_HALO
    run = pext_ref[...]
    diffs = []
    for gi, w in enumerate(POOL_WINDOWS):
        run = run + pltpu.roll(run, w // 2, 0)
        tot = run[POOL_HALO:pool_rows, 0:POOL_GD]
        cnt = jnp.minimum(t1, w).astype(F32)
        diffs.append(tot / cnt - u[:, gi * POOL_GD:(gi + 1) * POOL_GD])
        run = run[:, POOL_GD:]
    pext_ref[0:POOL_HALO, :] = pext_ref[tm:tm + POOL_HALO, :]
    dmix = jnp.concatenate(diffs, axis=1).astype(BF16)
    mixed = rows_dot(dmix, pbd_ref[...]) * ps_ref[...]
    ypool = rows_dot(mixed.astype(BF16), wpo_ref[...])

    part = gates_ref[:, D_MODEL:2 * D_MODEL] * yconv + gates_ref[:, 2 * D_MODEL:3 * D_MODEL] * ypool
    part_ref[0] = part.astype(BF16)
    g0_ref[0] = gates_ref[:, 0:D_MODEL].astype(BF16)


def _layer_spec(stacked, l):
    zeros = (0,) * (stacked.ndim - 1)
    return pl.BlockSpec((None,) + stacked.shape[1:], lambda b, i: (l,) + zeros, pipeline_mode=pl.Buffered(1))


def _mixer_in(x, cos_t, sin_t, p, l, *, tm):
    b, s, d = x.shape
    nt = s // tm
    row_tile = lambda w: pl.BlockSpec((1, tm, w), lambda bi, i: (bi, i, 0))
    col_tile = lambda r: pl.BlockSpec((1, r, tm), lambda bi, i: (bi, 0, i))
    assert s % tm == 0 and tm % MERGE_ROWS == 0 and tm % CONV_ROWS == 0
    consts = [p[name] for name in (
        "npre", "w_lat", "w_kr", "w_wide", "q_norm", "w_uq_t", "kv_norm", "w_uk", "w_uv_t", "conv_w", "conv_b",
        "ln_g", "ln_b", "w_conv_o", "pool_bd", "pool_scale", "w_pool_o")]
    return pl.pallas_call(
        functools.partial(_mixer_in_kernel, tm=tm),
        grid=(b, nt),
        in_specs=[row_tile(d), col_tile(ROPE_HALF), col_tile(ROPE_HALF)] + [_layer_spec(c, l) for c in consts],
        out_specs=[col_tile(N_HEADS * HEAD_PAD), row_tile(N_HEADS * HEAD_PAD), col_tile(N_HEADS * V_ROWS),
                   row_tile(d), row_tile(d)],
        out_shape=[
            jax.ShapeDtypeStruct((b, N_HEADS * HEAD_PAD, s), BF16),
            jax.ShapeDtypeStruct((b, s, N_HEADS * HEAD_PAD), BF16),
            jax.ShapeDtypeStruct((b, N_HEADS * V_ROWS, s), BF16),
            jax.ShapeDtypeStruct((b, s, d), BF16),
            jax.ShapeDtypeStruct((b, s, d), BF16),
        ],
        scratch_shapes=[
            pltpu.VMEM((tm + CONV_HALO, CONV_CH), F32),
            pltpu.VMEM((SUBLANES - 1, tm + CONV_HALO - SUBLANES, CONV_CH), F32),
            pltpu.VMEM((CONV_W, SUBLANES, CONV_CH), F32),
            pltpu.VMEM((tm, CONV_CH), F32),
            pltpu.VMEM((tm + POOL_HALO, POOL_CH), F32),
            pltpu.VMEM((tm, N_BRANCHES * D_MODEL), F32),
        ],
        compiler_params=pltpu.CompilerParams(
            dimension_semantics=("arbitrary", "arbitrary"), vmem_limit_bytes=VMEM_LIMIT_BYTES),
        name="mixer_in",
    )(x, cos_t, sin_t, *consts)


def _attn_kernel(qt_ref, k_ref, vt_ref, ot_ref, s0_ref, s1_ref, m_ref, acc_ref, *, tq):
    i = pl.program_id(1)
    half = tq // 2
    m_ref[...] = jnp.full(m_ref.shape, -jnp.inf, F32)
    acc_ref[...] = jnp.zeros(acc_ref.shape, F32)

    def head(h):
        return slice(h * HEAD_PAD, (h + 1) * HEAD_PAD)

    def scores_into(s_ref, c):
        k0 = pl.multiple_of(c * tq, tq)
        for h in range(N_HEADS):
            for r in range(0, tq, half):
                s_ref[h, r:r + half, :] = _dot(k_ref[0, pl.ds(k0 + r, half), head(h)], qt_ref[0, head(h), :])

    def update(s_ref, c, k_lo, n_keys, q_lo, masked):
        k0 = pl.multiple_of(c * tq + k_lo, half)
        for h in range(N_HEADS):
            s = s_ref[h, k_lo:k_lo + n_keys, q_lo:tq]
            if masked:
                key = lax.broadcasted_iota(jnp.int32, (n_keys, 1), 0) + (k_lo - q_lo)
                qry = lax.broadcasted_iota(jnp.int32, (1, tq - q_lo), 1)
                s = jnp.where(key <= qry, s, -jnp.inf)
            m = m_ref[h, :, q_lo:tq]
            m_new = jnp.maximum(m, jnp.max(s, axis=0, keepdims=True))
            alpha = jnp.exp2(m - m_new)
            prob = jnp.exp2(s - m_new).astype(BF16)
            m_ref[h, :, q_lo:tq] = m_new
            vc = vt_ref[0, h * V_ROWS:(h + 1) * V_ROWS, pl.ds(k0, n_keys)]
            acc_ref[h, :, q_lo:tq] = alpha * acc_ref[h, :, q_lo:tq] + _dot(vc, prob)

    def diagonal(s_ref):
        update(s_ref, i, 0, half, 0, True)
        update(s_ref, i, half, half, half, True)

    def pair(p, carry):
        scores_into(s1_ref, 2 * p + 1)
        update(s0_ref, 2 * p, 0, tq, 0, False)
        scores_into(s0_ref, 2 * p + 2)
        update(s1_ref, 2 * p + 1, 0, tq, 0, False)
        return carry

    def tail_even():
        diagonal(s0_ref)

    def tail_odd():
        scores_into(s1_ref, i)
        update(s0_ref, i - 1, 0, tq, 0, False)
        diagonal(s1_ref)

    @pl.when(i == 0)
    def _():
        scores_into(s0_ref, 0)
        tail_even()

    @pl.when(i == 1)
    def _():
        scores_into(s0_ref, 0)
        tail_odd()

    @pl.when(i >= 2)
    def _():
        scores_into(s0_ref, 0)
        pair(0, 0)

    lax.fori_loop(1, i // 2, pair, 0)

    @pl.when(jnp.logical_and(i >= 2, i % 2 == 0))
    def _():
        tail_even()

    @pl.when(jnp.logical_and(i >= 2, i % 2 == 1))
    def _():
        tail_odd()

    for h in range(N_HEADS):
        ot_ref[0, h * V_DIM:(h + 1) * V_DIM, :] = (acc_ref[h, 0:V_DIM, :] / acc_ref[h, V_DIM:V_DIM + 1, :]).astype(BF16)


def _attention(q_t, k, v_t, *, tq):
    b, _, s = q_t.shape
    tk = tq
    assert s % tq == 0
    return pl.pallas_call(
        functools.partial(_attn_kernel, tq=tq),
        grid=(b, s // tq),
        in_specs=[
            pl.BlockSpec((1, N_HEADS * HEAD_PAD, tq), lambda bi, i: (bi, 0, i)),
            pl.BlockSpec((1, s, N_HEADS * HEAD_PAD), lambda bi, i: (bi, 0, 0)),
            pl.BlockSpec((1, N_HEADS * V_ROWS, s), lambda bi, i: (bi, 0, 0)),
        ],
        out_specs=pl.BlockSpec((1, N_HEADS * V_DIM, tq), lambda bi, i: (bi, 0, i)),
        out_shape=jax.ShapeDtypeStruct((b, N_HEADS * V_DIM, s), BF16),
        scratch_shapes=[
            pltpu.VMEM((N_HEADS, tk, tq), F32),
            pltpu.VMEM((N_HEADS, tk, tq), F32),
            pltpu.VMEM((N_HEADS, 1, tq), F32),
            pltpu.VMEM((N_HEADS, V_ROWS, tq), F32),
        ],
        compiler_params=pltpu.CompilerParams(
            dimension_semantics=("arbitrary", "arbitrary"), vmem_limit_bytes=VMEM_LIMIT_BYTES),
        name="attention",
    )(q_t, k, v_t)


def _merge_ffn_kernel(h_ref, ot_ref, g0_ref, part_ref, wao_ref, wmo_ref, npost_ref, fpre_ref, wg_ref, wu_ref,
                      wd_ref, fpost_ref, out_ref):
    tm = h_ref.shape[1]
    groups = [slice(r, r + MERGE_ROWS) for r in range(0, tm, MERGE_ROWS)]
    h1 = []
    for rows in groups:
        yattn = _dot(ot_ref[0, :, rows], wao_ref[...], _TN)
        merged = g0_ref[0, rows, :].astype(F32) * yattn + part_ref[0, rows, :].astype(F32)
        h1.append(h_ref[0, rows, :] + _rms(_dot(merged.astype(BF16), wmo_ref[...]), npost_ref[...]))
    act = []
    for g in range(len(groups)):
        hn = _rms(h1[g], fpre_ref[...]).astype(BF16)
        gate = _dot(hn, wg_ref[...])
        act.append((gate * jax.nn.sigmoid(gate) * _dot(hn, wu_ref[...])).astype(BF16))
    for g, rows in enumerate(groups):
        out_ref[0, rows, :] = h1[g] + _rms(_dot(act[g], wd_ref[...]), fpost_ref[...])


def _merge_ffn(h, o_t, g0, part, p, l, *, tm):
    b, s, d = h.shape
    row_tile = lambda w: pl.BlockSpec((1, tm, w), lambda bi, i: (bi, i, 0))
    assert s % tm == 0 and tm % MERGE_ROWS == 0
    consts = [p["w_attn_o"], p["w_mix_o"], p["npost"], p["fpre"], p["w_gate"], p["w_up"], p["w_down"], p["fpost"]]
    return pl.pallas_call(
        _merge_ffn_kernel,
        grid=(b, s // tm),
        in_specs=[row_tile(d), pl.BlockSpec((1, N_HEADS * V_DIM, tm), lambda bi, i: (bi, 0, i)), row_tile(d),
                  row_tile(d)] + [_layer_spec(c, l) for c in consts],
        out_specs=row_tile(d),
        out_shape=jax.ShapeDtypeStruct((b, s, d), F32),
        compiler_params=pltpu.CompilerParams(
            dimension_semantics=("arbitrary", "arbitrary"), vmem_limit_bytes=VMEM_LIMIT_BYTES),
        name="merge_ffn",
    )(h, o_t, g0, part, *consts)


def _stacked_params(mix_norm_pre, w_in, q_norm, w_uq, kv_norm, w_uk, w_uv, w_attn_o, conv_w, conv_b, conv_ln_g,
                    conv_ln_b, w_conv_o, pool_w, pool_scale, w_pool_o, w_mix_o, mix_norm_post, ffn_norm_pre, w_gate,
                    w_up, w_down, ffn_norm_post):
    depth = w_in.shape[0]
    row = lambda v: v.reshape(depth, 1, -1)
    w_in = w_in.astype(BF16)
    w_kr = jnp.pad(w_in[:, :, LAT_COLS:WIDE_START], ((0, 0), (0, 0), (0, HEAD_PAD - ROPE)))
    assert w_in.shape[2] - WIDE_START == SEG_GATES[1]

    def pad_heads(w, dh, dh_pad):
        w = jnp.pad(w.reshape(depth, w.shape[1], N_HEADS, dh), ((0, 0), (0, 0), (0, 0), (0, dh_pad - dh)))
        return w.reshape(depth, w.shape[1], N_HEADS * dh_pad)

    transpose = lambda w: jnp.swapaxes(w, 1, 2)
    pool_bd = jnp.zeros((depth, POOL_CH, POOL_CH), F32)
    for gi in range(len(POOL_WINDOWS)):
        sl = slice(gi * POOL_GD, (gi + 1) * POOL_GD)
        pool_bd = pool_bd.at[:, sl, sl].set(pool_w[:, gi])
    return dict(
        npre=row(mix_norm_pre), w_lat=w_in[:, :, :LAT_COLS], w_kr=w_kr,
        w_wide=w_in[:, :, WIDE_START:], q_norm=row(q_norm),
        w_uq_t=transpose(pad_heads(w_uq, QK_DIM, HEAD_PAD)).astype(BF16), kv_norm=row(kv_norm),
        w_uk=pad_heads(w_uk, NOPE, HEAD_PAD).astype(BF16),
        w_uv_t=transpose(pad_heads(w_uv, V_DIM, V_ROWS)).astype(BF16),
        conv_w=conv_w, conv_b=row(conv_b), ln_g=row(conv_ln_g), ln_b=row(conv_ln_b),
        w_conv_o=w_conv_o.astype(BF16), pool_bd=pool_bd.astype(BF16), pool_scale=row(pool_scale),
        w_pool_o=w_pool_o.astype(BF16),
        w_attn_o=w_attn_o.astype(BF16), w_mix_o=w_mix_o.astype(BF16), npost=row(mix_norm_post),
        fpre=row(ffn_norm_pre), w_gate=w_gate.astype(BF16), w_up=w_up.astype(BF16),
        w_down=w_down.astype(BF16), fpost=row(ffn_norm_post),
    )


def kernel(x, positions, mix_norm_pre, w_in, q_norm, w_uq, kv_norm, w_uk, w_uv, w_attn_o, conv_w, conv_b, conv_ln_g, conv_ln_b, w_conv_o, pool_w, pool_scale, w_pool_o, w_mix_o, mix_norm_post, ffn_norm_pre, w_gate, w_up, w_down, ffn_norm_post):
    p = _stacked_params(mix_norm_pre, w_in, q_norm, w_uq, kv_norm, w_uk, w_uv, w_attn_o, conv_w, conv_b, conv_ln_g,
                        conv_ln_b, w_conv_o, pool_w, pool_scale, w_pool_o, w_mix_o, mix_norm_post, ffn_norm_pre,
                        w_gate, w_up, w_down, ffn_norm_post)
    assert x.dtype == F32 and x.shape[2] == D_MODEL and positions.shape == x.shape[:2]
    s = x.shape[1]
    tm = min(ROW_TILE, s)
    cos_t, sin_t = _rope_tables(positions)
    h = x
    for l in range(w_in.shape[0]):
        q_t, k, v_t, g0, part = _mixer_in(h, cos_t, sin_t, p, l, tm=tm)
        o_t = _attention(q_t, k, v_t, tq=min(ATTN_TQ, s))
        h = _merge_ffn(h, o_t, g0, part, p, l, tm=tm)
    return h
```

```python
import functools
import math

import jax
import jax.numpy as jnp
from jax import lax
from jax.experimental import pallas as pl
from jax.experimental.pallas import tpu as pltpu

D_MODEL = 1024
N_HEADS = 8
NOPE = 64
ROPE = 32
ROPE_HALF = ROPE // 2
V_DIM = 64
Q_RANK = 384
KV_RANK = 256
ROPE_THETA = 10000.0
CONV_CH = 512
CONV_W = 31
POOL_WINDOWS = (2, 4, 8, 16)
POOL_CH = 512
POOL_GD = POOL_CH // len(POOL_WINDOWS)
N_BRANCHES = 3
D_FF = 2816
EPS = 1e-6

LANES = 128
SUBLANES = 8
BF16_ROWS = 16
MXU_DIM = 256
VMEM_LIMIT_BYTES = 56 * 1024 * 1024

HEAD_PAD = LANES
QK_DIM = NOPE + ROPE
V_ROWS = V_DIM + BF16_ROWS
Q_SCALE = (1.0 / math.sqrt(QK_DIM)) * math.log2(math.e)

LAT_COLS = Q_RANK + KV_RANK
WIDE_START = LAT_COLS + ROPE
SEG_Q = (0, Q_RANK)
SEG_KV = (Q_RANK, LAT_COLS)
SEG_CA = (0, CONV_CH)
SEG_CG = (CONV_CH, 2 * CONV_CH)
SEG_POOL = (2 * CONV_CH, 2 * CONV_CH + POOL_CH)
SEG_GATES = (SEG_POOL[1], SEG_POOL[1] + N_BRANCHES * D_MODEL)

CONV_HALO = 32
POOL_HALO = 16
CONV_ROWS = 32
WIDE_COPY_COLS = 512

ROW_TILE = 512
MERGE_ROWS = 256
ATTN_TQ = 512

F32 = jnp.float32
BF16 = jnp.bfloat16

_NN = (((1,), (0,)), ((), ()))
_NT = (((1,), (1,)), ((), ()))
_TN = (((0,), (0,)), ((), ()))


def _rms(x, g):
    return x * lax.rsqrt(jnp.mean(x * x, axis=-1, keepdims=True) + EPS) * g


def _dot(a, b, dims=_NN):
    return lax.dot_general(a, b, dims, preferred_element_type=F32)


def _rope_rows(x1, x2, cos, sin):
    return x1 * cos - x2 * sin, x2 * cos + x1 * sin


def _rope_table_kernel(pos_ref, invf_ref, cos_ref, sin_ref):
    ang = invf_ref[...] * pos_ref[0].astype(F32)
    cos_ref[0] = jnp.cos(ang)
    sin_ref[0] = jnp.sin(ang)


def _rope_tables(positions):
    b, s = positions.shape
    inv_freq = ROPE_THETA ** (-jnp.arange(0, ROPE, 2, dtype=F32) / ROPE)
    out = jax.ShapeDtypeStruct((b, ROPE_HALF, s), F32)
    return pl.pallas_call(
        _rope_table_kernel,
        grid=(b,),
        in_specs=[
            pl.BlockSpec((1, 1, s), lambda i: (i, 0, 0)),
            pl.BlockSpec((ROPE_HALF, 1), lambda i: (0, 0)),
        ],
        out_specs=[
            pl.BlockSpec((1, ROPE_HALF, s), lambda i: (i, 0, 0)),
            pl.BlockSpec((1, ROPE_HALF, s), lambda i: (i, 0, 0)),
        ],
        out_shape=[out, out],
        name="rope_tables",
    )(positions.reshape(b, 1, s), inv_freq.reshape(ROPE_HALF, 1))


def _mixer_in_kernel(x_ref, cos_ref, sin_ref, npre_ref, win_ref, qn_ref, wuqt_ref, kvn_ref,
                     wuk_ref, wuvt_ref, cw_ref, cb_ref, lng_ref, lnb_ref, wco_ref, pbd_ref, ps_ref, wpo_ref,
                     qt_ref, k_ref, vt_ref, g0_ref, part_ref,
                     wwide_ref, cext_ref, cshift_ref, cwb_ref, conv_ref, pext_ref, gates_ref, *, tm):
    i = pl.program_id(1)
    wlat_ref = win_ref

    @pl.when(jnp.logical_and(pl.program_id(0) == 0, i == 0))
    def _():
        for c in range(0, SEG_GATES[1], WIDE_COPY_COLS):
            wwide_ref[:, c:c + WIDE_COPY_COLS] = win_ref[:, WIDE_START + c:WIDE_START + c + WIDE_COPY_COLS]
    hn = [_rms(x_ref[0, r:r + MERGE_ROWS, :], npre_ref[...]).astype(BF16) for r in range(0, tm, MERGE_ROWS)]

    def proj(w_ref, lo, hi):
        return jnp.concatenate([_dot(g, w_ref[:, lo:hi]) for g in hn], axis=0)

    def rows_dot(a, w):
        return jnp.concatenate([_dot(a[r:r + MERGE_ROWS], w) for r in range(0, tm, MERGE_ROWS)], axis=0)

    @pl.when(i == 0)
    def _():
        cext_ref[0:CONV_HALO, :] = jnp.zeros((CONV_HALO, CONV_CH), F32)
        pext_ref[0:POOL_HALO, :] = jnp.zeros((POOL_HALO, POOL_CH), F32)
        for j in range(CONV_W):
            cwb_ref[j] = jnp.broadcast_to(cw_ref[j:j + 1, :], (SUBLANES, CONV_CH))

    cext_ref[CONV_HALO:CONV_HALO + tm, :] = proj(wwide_ref, *SEG_CA) * jax.nn.sigmoid(proj(wwide_ref, *SEG_CG))
    ext_rows = tm + CONV_HALO
    ext = cext_ref[...]
    for r in range(1, SUBLANES):
        cshift_ref[r - 1] = pltpu.roll(ext, ext_rows - r, 0)[0:ext_rows - SUBLANES]
    first_shift = CONV_HALO - (CONV_W - 1)
    n_conv_steps = tm // CONV_ROWS
    gate_slabs = (SEG_GATES[1] - SEG_GATES[0]) // MXU_DIM
    slab = 0
    for step in range(n_conv_steps):
        r0 = step * CONV_ROWS
        accs = [None] * (CONV_ROWS // SUBLANES)
        for j in range(CONV_W):
            s = first_shift + j
            r, base = s % SUBLANES, r0 + s - s % SUBLANES
            w_tile = cwb_ref[j]
            for g in range(len(accs)):
                rows = slice(base + g * SUBLANES, base + (g + 1) * SUBLANES)
                src = cext_ref[rows, :] if r == 0 else cshift_ref[r - 1, rows, :]
                accs[g] = src * w_tile if accs[g] is None else accs[g] + src * w_tile
        conv_ref[r0:r0 + CONV_ROWS, :] = jnp.concatenate(accs, axis=0)
        slab_end = (step + 1) * gate_slabs // n_conv_steps
        while slab < slab_end:
            lo = slab * MXU_DIM
            gates_ref[:, lo:lo + MXU_DIM] = jax.nn.sigmoid(
                proj(wwide_ref, SEG_GATES[0] + lo, SEG_GATES[0] + lo + MXU_DIM))
            slab += 1
    cext_ref[0:CONV_HALO, :] = cext_ref[tm:tm + CONV_HALO, :]

    cos = cos_ref[0]
    sin = sin_ref[0]

    cq = _rms(proj(wlat_ref, *SEG_Q), qn_ref[...]).astype(BF16)
    qt = _dot(wuqt_ref[...], cq, _NT)
    for h in range(N_HEADS):
        base = h * HEAD_PAD
        r1, r2 = _rope_rows(qt[base + NOPE:base + NOPE + ROPE_HALF], qt[base + NOPE + ROPE_HALF:base + QK_DIM],
                            cos, sin)
        blk = jnp.concatenate([qt[base:base + NOPE], r1, r2, qt[base + QK_DIM:base + HEAD_PAD]], axis=0)
        qt_ref[0, base:base + HEAD_PAD, :] = (blk * Q_SCALE).astype(BF16)

    ckv = _rms(proj(wlat_ref, *SEG_KV), kvn_ref[...]).astype(BF16)
    kpad = rows_dot(ckv, wuk_ref[...])
    row_id = lax.broadcasted_iota(jnp.int32, (N_HEADS * V_ROWS, 1), 0)
    ones_row = sum((row_id == h * V_ROWS + V_DIM).astype(F32) for h in range(N_HEADS))
    vt_ref[0] = (_dot(wuvt_ref[...], ckv, _NT) + ones_row).astype(BF16)
    krt = proj(win_ref, LAT_COLS, LAT_COLS + HEAD_PAD).T
    r1, r2 = _rope_rows(krt[0:ROPE_HALF], krt[ROPE_HALF:ROPE], cos, sin)
    kr = jnp.concatenate([jnp.zeros((NOPE, tm), F32), r1, r2, jnp.zeros((HEAD_PAD - QK_DIM, tm), F32)], axis=0).T
    for h in range(N_HEADS):
        base = h * HEAD_PAD
        k_ref[0, :, base:base + HEAD_PAD] = (kpad[:, base:base + HEAD_PAD] + kr).astype(BF16)

    hc = conv_ref[...] + cb_ref[...]
    xc = hc - jnp.mean(hc, axis=-1, keepdims=True)
    ln = xc * lax.rsqrt(jnp.mean(xc * xc, axis=-1, keepdims=True) + EPS) * lng_ref[...] + lnb_ref[...]
    yconv = rows_dot((ln * jax.nn.sigmoid(ln)).astype(BF16), wco_ref[...])

    u = proj(wwide_ref, *SEG_POOL)
    pext_ref[POOL_HALO:POOL_HALO + tm, :] = u
    t1 = i * tm + 1 + lax.broadcasted_iota(jnp.int32, (tm, 1), 0)
    pool_rows = tm + POOL_HALO
    run = pext_ref[...]
    diffs = []
    for gi, w in enumerate(POOL_WINDOWS):
        run = run + pltpu.roll(run, w // 2, 0)
        tot = run[POOL_HALO:pool_rows, 0:POOL_GD]
        cnt = jnp.minimum(t1, w).astype(F32)
        diffs.append(tot / cnt - u[:, gi * POOL_GD:(gi + 1) * POOL_GD])
        run = run[:, POOL_GD:]
    pext_ref[0:POOL_HALO, :] = pext_ref[tm:tm + POOL_HALO, :]
    dmix = jnp.concatenate(diffs, axis=1).astype(BF16)
    mixed = rows_dot(dmix, pbd_ref[...]) * ps_ref[...]
    ypool = rows_dot(mixed.astype(BF16), wpo_ref[...])

    part = gates_ref[:, D_MODEL:2 * D_MODEL] * yconv + gates_ref[:, 2 * D_MODEL:3 * D_MODEL] * ypool
    part_ref[0] = part.astype(BF16)
    g0_ref[0] = gates_ref[:, 0:D_MODEL].astype(BF16)


def _layer_spec(stacked, l):
    zeros = (0,) * (stacked.ndim - 1)
    return pl.BlockSpec((None,) + stacked.shape[1:], lambda b, i: (l,) + zeros, pipeline_mode=pl.Buffered(1))


def _mixer_in(x, cos_t, sin_t, p, l, *, tm):
    b, s, d = x.shape
    nt = s // tm
    row_tile = lambda w: pl.BlockSpec((1, tm, w), lambda bi, i: (bi, i, 0))
    col_tile = lambda r: pl.BlockSpec((1, r, tm), lambda bi, i: (bi, 0, i))
    assert s % tm == 0 and tm % MERGE_ROWS == 0 and tm % CONV_ROWS == 0
    consts = [p[name] for name in (
        "npre", "w_in", "q_norm", "w_uq_t", "kv_norm", "w_uk", "w_uv_t", "conv_w", "conv_b",
        "ln_g", "ln_b", "w_conv_o", "pool_bd", "pool_scale", "w_pool_o")]
    return pl.pallas_call(
        functools.partial(_mixer_in_kernel, tm=tm),
        grid=(b, nt),
        in_specs=[row_tile(d), col_tile(ROPE_HALF), col_tile(ROPE_HALF)] + [_layer_spec(c, l) for c in consts],
        out_specs=[col_tile(N_HEADS * HEAD_PAD), row_tile(N_HEADS * HEAD_PAD), col_tile(N_HEADS * V_ROWS),
                   row_tile(d), row_tile(d)],
        out_shape=[
            jax.ShapeDtypeStruct((b, N_HEADS * HEAD_PAD, s), BF16),
            jax.ShapeDtypeStruct((b, s, N_HEADS * HEAD_PAD), BF16),
            jax.ShapeDtypeStruct((b, N_HEADS * V_ROWS, s), BF16),
            jax.ShapeDtypeStruct((b, s, d), BF16),
            jax.ShapeDtypeStruct((b, s, d), BF16),
        ],
        scratch_shapes=[
            pltpu.VMEM((d, SEG_GATES[1]), BF16),
            pltpu.VMEM((tm + CONV_HALO, CONV_CH), F32),
            pltpu.VMEM((SUBLANES - 1, tm + CONV_HALO - SUBLANES, CONV_CH), F32),
            pltpu.VMEM((CONV_W, SUBLANES, CONV_CH), F32),
            pltpu.VMEM((tm, CONV_CH), F32),
            pltpu.VMEM((tm + POOL_HALO, POOL_CH), F32),
            pltpu.VMEM((tm, N_BRANCHES * D_MODEL), F32),
        ],
        compiler_params=pltpu.CompilerParams(
            dimension_semantics=("arbitrary", "arbitrary"), vmem_limit_bytes=VMEM_LIMIT_BYTES),
        name="mixer_in",
    )(x, cos_t, sin_t, *consts)


def _attn_kernel(qt_ref, k_ref, vt_ref, ot_ref, s0_ref, s1_ref, m_ref, acc_ref, *, tq):
    i = pl.program_id(1)
    half = tq // 2
    m_ref[...] = jnp.full(m_ref.shape, -jnp.inf, F32)
    acc_ref[...] = jnp.zeros(acc_ref.shape, F32)

    def head(h):
        return slice(h * HEAD_PAD, (h + 1) * HEAD_PAD)

    def scores_into(s_ref, c):
        k0 = pl.multiple_of(c * tq, tq)
        for h in range(N_HEADS):
            s_ref[h] = _dot(k_ref[0, pl.ds(k0, tq), head(h)], qt_ref[0, head(h), :])

    def update(s_ref, c, k_lo, n_keys, q_lo, masked):
        k0 = pl.multiple_of(c * tq + k_lo, half)
        for h in range(N_HEADS):
            s = s_ref[h, k_lo:k_lo + n_keys, q_lo:tq]
            if masked:
                key = lax.broadcasted_iota(jnp.int32, (n_keys, 1), 0) + (k_lo - q_lo)
                qry = lax.broadcasted_iota(jnp.int32, (1, tq - q_lo), 1)
                s = jnp.where(key <= qry, s, -jnp.inf)
            m = m_ref[h, :, q_lo:tq]
            m_new = jnp.maximum(m, jnp.max(s, axis=0, keepdims=True))
            alpha = jnp.exp2(m - m_new)
            prob = jnp.exp2(s - m_new).astype(BF16)
            m_ref[h, :, q_lo:tq] = m_new
            vc = vt_ref[0, h * V_ROWS:(h + 1) * V_ROWS, pl.ds(k0, n_keys)]
            acc_ref[h, :, q_lo:tq] = alpha * acc_ref[h, :, q_lo:tq] + _dot(vc, prob)

    def diagonal(s_ref):
        update(s_ref, i, 0, half, 0, True)
        update(s_ref, i, half, half, half, True)

    def pair(p, carry):
        scores_into(s1_ref, 2 * p + 1)
        update(s0_ref, 2 * p, 0, tq, 0, False)
        scores_into(s0_ref, 2 * p + 2)
        update(s1_ref, 2 * p + 1, 0, tq, 0, False)
        return carry

    def tail_even():
        diagonal(s0_ref)

    def tail_odd():
        scores_into(s1_ref, i)
        update(s0_ref, i - 1, 0, tq, 0, False)
        diagonal(s1_ref)

    @pl.when(i == 0)
    def _():
        scores_into(s0_ref, 0)
        tail_even()

    @pl.when(i == 1)
    def _():
        scores_into(s0_ref, 0)
        tail_odd()

    @pl.when(i >= 2)
    def _():
        scores_into(s0_ref, 0)
        pair(0, 0)

    lax.fori_loop(1, i // 2, pair, 0)

    @pl.when(jnp.logical_and(i >= 2, i % 2 == 0))
    def _():
        tail_even()

    @pl.when(jnp.logical_and(i >= 2, i % 2 == 1))
    def _():
        tail_odd()

    for h in range(N_HEADS):
        ot_ref[0, h * V_DIM:(h + 1) * V_DIM, :] = (acc_ref[h, 0:V_DIM, :] / acc_ref[h, V_DIM:V_DIM + 1, :]).astype(BF16)


def _attention(q_t, k, v_t, *, tq):
    b, _, s = q_t.shape
    tk = tq
    assert s % tq == 0
    return pl.pallas_call(
        functools.partial(_attn_kernel, tq=tq),
        grid=(b, s // tq),
        in_specs=[
            pl.BlockSpec((1, N_HEADS * HEAD_PAD, tq), lambda bi, i: (bi, 0, i)),
            pl.BlockSpec((1, s, N_HEADS * HEAD_PAD), lambda bi, i: (bi, 0, 0)),
            pl.BlockSpec((1, N_HEADS * V_ROWS, s), lambda bi, i: (bi, 0, 0)),
        ],
        out_specs=pl.BlockSpec((1, N_HEADS * V_DIM, tq), lambda bi, i: (bi, 0, i)),
        out_shape=jax.ShapeDtypeStruct((b, N_HEADS * V_DIM, s), BF16),
        scratch_shapes=[
            pltpu.VMEM((N_HEADS, tk, tq), F32),
            pltpu.VMEM((N_HEADS, tk, tq), F32),
            pltpu.VMEM((N_HEADS, 1, tq), F32),
            pltpu.VMEM((N_HEADS, V_ROWS, tq), F32),
        ],
        compiler_params=pltpu.CompilerParams(
            dimension_semantics=("arbitrary", "arbitrary"), vmem_limit_bytes=VMEM_LIMIT_BYTES),
        name="attention",
    )(q_t, k, v_t)


def _merge_ffn_kernel(h_ref, ot_ref, g0_ref, part_ref, wao_ref, wmo_ref, npost_ref, fpre_ref, wg_ref, wu_ref,
                      wd_ref, fpost_ref, out_ref):
    tm = h_ref.shape[1]
    groups = [slice(r, r + MERGE_ROWS) for r in range(0, tm, MERGE_ROWS)]
    h1 = []
    for rows in groups:
        yattn = _dot(ot_ref[0, :, rows], wao_ref[...], _TN)
        merged = g0_ref[0, rows, :].astype(F32) * yattn + part_ref[0, rows, :].astype(F32)
        h1.append(h_ref[0, rows, :] + _rms(_dot(merged.astype(BF16), wmo_ref[...]), npost_ref[...]))
    act = []
    for g in range(len(groups)):
        hn = _rms(h1[g], fpre_ref[...]).astype(BF16)
        gate = _dot(hn, wg_ref[...])
        act.append((gate * jax.nn.sigmoid(gate) * _dot(hn, wu_ref[...])).astype(BF16))
    for g, rows in enumerate(groups):
        out_ref[0, rows, :] = h1[g] + _rms(_dot(act[g], wd_ref[...]), fpost_ref[...])


def _merge_ffn(h, o_t, g0, part, p, l, *, tm):
    b, s, d = h.shape
    row_tile = lambda w: pl.BlockSpec((1, tm, w), lambda bi, i: (bi, i, 0))
    assert s % tm == 0 and tm % MERGE_ROWS == 0
    consts = [p["w_attn_o"], p["w_mix_o"], p["npost"], p["fpre"], p["w_gate"], p["w_up"], p["w_down"], p["fpost"]]
    return pl.pallas_call(
        _merge_ffn_kernel,
        grid=(b, s // tm),
        in_specs=[row_tile(d), pl.BlockSpec((1, N_HEADS * V_DIM, tm), lambda bi, i: (bi, 0, i)), row_tile(d),
                  row_tile(d)] + [_layer_spec(c, l) for c in consts],
        out_specs=row_tile(d),
        out_shape=jax.ShapeDtypeStruct((b, s, d), F32),
        compiler_params=pltpu.CompilerParams(
            dimension_semantics=("arbitrary", "arbitrary"), vmem_limit_bytes=VMEM_LIMIT_BYTES),
        name="merge_ffn",
    )(h, o_t, g0, part, *consts)


def _stacked_params(mix_norm_pre, w_in, q_norm, w_uq, kv_norm, w_uk, w_uv, w_attn_o, conv_w, conv_b, conv_ln_g,
                    conv_ln_b, w_conv_o, pool_w, pool_scale, w_pool_o, w_mix_o, mix_norm_post, ffn_norm_pre, w_gate,
                    w_up, w_down, ffn_norm_post):
    depth = w_in.shape[0]
    row = lambda v: v.reshape(depth, 1, -1)
    assert w_in.shape[2] - WIDE_START == SEG_GATES[1] and SEG_GATES[1] % WIDE_COPY_COLS == 0

    def pad_heads(w, dh, dh_pad):
        w = jnp.pad(w.reshape(depth, w.shape[1], N_HEADS, dh), ((0, 0), (0, 0), (0, 0), (0, dh_pad - dh)))
        return w.reshape(depth, w.shape[1], N_HEADS * dh_pad)

    transpose = lambda w: jnp.swapaxes(w, 1, 2)
    pool_bd = jnp.zeros((depth, POOL_CH, POOL_CH), F32)
    for gi in range(len(POOL_WINDOWS)):
        sl = slice(gi * POOL_GD, (gi + 1) * POOL_GD)
        pool_bd = pool_bd.at[:, sl, sl].set(pool_w[:, gi])
    return dict(
        npre=row(mix_norm_pre), w_in=w_in.astype(BF16), q_norm=row(q_norm),
        w_uq_t=transpose(pad_heads(w_uq, QK_DIM, HEAD_PAD)).astype(BF16), kv_norm=row(kv_norm),
        w_uk=pad_heads(w_uk, NOPE, HEAD_PAD).astype(BF16),
        w_uv_t=transpose(pad_heads(w_uv, V_DIM, V_ROWS)).astype(BF16),
        conv_w=conv_w, conv_b=row(conv_b), ln_g=row(conv_ln_g), ln_b=row(conv_ln_b),
        w_conv_o=w_conv_o.astype(BF16), pool_bd=pool_bd.astype(BF16), pool_scale=row(pool_scale),
        w_pool_o=w_pool_o.astype(BF16),
        w_attn_o=w_attn_o.astype(BF16), w_mix_o=w_mix_o.astype(BF16), npost=row(mix_norm_post),
        fpre=row(ffn_norm_pre), w_gate=w_gate.astype(BF16), w_up=w_up.astype(BF16),
        w_down=w_down.astype(BF16), fpost=row(ffn_norm_post),
    )


def kernel(x, positions, mix_norm_pre, w_in, q_norm, w_uq, kv_norm, w_uk, w_uv, w_attn_o, conv_w, conv_b, conv_ln_g, conv_ln_b, w_conv_o, pool_w, pool_scale, w_pool_o, w_mix_o, mix_norm_post, ffn_norm_pre, w_gate, w_up, w_down, ffn_norm_post):
    p = _stacked_params(mix_norm_pre, w_in, q_norm, w_uq, kv_norm, w_uk, w_uv, w_attn_o, conv_w, conv_b, conv_ln_g,
                        conv_ln_b, w_conv_o, pool_w, pool_scale, w_pool_o, w_mix_o, mix_norm_post, ffn_norm_pre,
                        w_gate, w_up, w_down, ffn_norm_post)
    assert x.dtype == F32 and x.shape[2] == D_MODEL and positions.shape == x.shape[:2]
    s = x.shape[1]
    tm = min(ROW_TILE, s)
    cos_t, sin_t = _rope_tables(positions)
    h = x
    for l in range(w_in.shape[0]):
        q_t, k, v_t, g0, part = _mixer_in(h, cos_t, sin_t, p, l, tm=tm)
        o_t = _attention(q_t, k, v_t, tq=min(ATTN_TQ, s))
        h = _merge_ffn(h, o_t, g0, part, p, l, tm=tm)
    return h
```

```python
import functools
import math

import jax
import jax.numpy as jnp
from jax import lax
from jax.experimental import pallas as pl
from jax.experimental.pallas import tpu as pltpu

D_MODEL = 1024
N_HEADS = 8
NOPE = 64
ROPE = 32
ROPE_HALF = ROPE // 2
V_DIM = 64
Q_RANK = 384
KV_RANK = 256
ROPE_THETA = 10000.0
CONV_CH = 512
CONV_W = 31
POOL_WINDOWS = (2, 4, 8, 16)
POOL_CH = 512
POOL_GD = POOL_CH // len(POOL_WINDOWS)
N_BRANCHES = 3
D_FF = 2816
EPS = 1e-6

LANES = 128
SUBLANES = 8
BF16_ROWS = 16
MXU_DIM = 256
VMEM_LIMIT_BYTES = 56 * 1024 * 1024

HEAD_PAD = LANES
QK_DIM = NOPE + ROPE
V_ROWS = V_DIM + BF16_ROWS
Q_SCALE = (1.0 / math.sqrt(QK_DIM)) * math.log2(math.e)

LAT_COLS = Q_RANK + KV_RANK
WIDE_START = LAT_COLS + ROPE
SEG_Q = (0, Q_RANK)
SEG_KV = (Q_RANK, LAT_COLS)
SEG_CA = (0, CONV_CH)
SEG_CG = (CONV_CH, 2 * CONV_CH)
SEG_POOL = (2 * CONV_CH, 2 * CONV_CH + POOL_CH)
SEG_GATES = (SEG_POOL[1], SEG_POOL[1] + N_BRANCHES * D_MODEL)

CONV_HALO = 32
POOL_HALO = 16
CONV_ROWS = 32
WIDE_COPY_COLS = 512

ROW_TILE = 512
MERGE_ROWS = 256
ATTN_TQ = 512

F32 = jnp.float32
BF16 = jnp.bfloat16

_NN = (((1,), (0,)), ((), ()))
_NT = (((1,), (1,)), ((), ()))
_TN = (((0,), (0,)), ((), ()))


def _rms(x, g):
    return x * lax.rsqrt(jnp.mean(x * x, axis=-1, keepdims=True) + EPS) * g


def _dot(a, b, dims=_NN):
    return lax.dot_general(a, b, dims, preferred_element_type=F32)


def _rope_rows(x1, x2, cos, sin):
    return x1 * cos - x2 * sin, x2 * cos + x1 * sin


def _rope_table_kernel(pos_ref, invf_ref, cos_ref, sin_ref):
    ang = invf_ref[...] * pos_ref[0].astype(F32)
    cos_ref[0] = jnp.cos(ang)
    sin_ref[0] = jnp.sin(ang)


def _rope_tables(positions):
    b, s = positions.shape
    inv_freq = ROPE_THETA ** (-jnp.arange(0, ROPE, 2, dtype=F32) / ROPE)
    out = jax.ShapeDtypeStruct((b, ROPE_HALF, s), F32)
    return pl.pallas_call(
        _rope_table_kernel,
        grid=(b,),
        in_specs=[
            pl.BlockSpec((1, 1, s), lambda i: (i, 0, 0)),
            pl.BlockSpec((ROPE_HALF, 1), lambda i: (0, 0)),
        ],
        out_specs=[
            pl.BlockSpec((1, ROPE_HALF, s), lambda i: (i, 0, 0)),
            pl.BlockSpec((1, ROPE_HALF, s), lambda i: (i, 0, 0)),
        ],
        out_shape=[out, out],
        name="rope_tables",
    )(positions.reshape(b, 1, s), inv_freq.reshape(ROPE_HALF, 1))


def _mixer_in_kernel(x_ref, cos_ref, sin_ref, npre_ref, win_ref, qn_ref, wuqt_ref, kvn_ref,
                     wuk_ref, wuvt_ref, cw_ref, cb_ref, lng_ref, lnb_ref, wco_ref, pbd_ref, ps_ref, wpo_ref,
                     qt_ref, k_ref, vt_ref, g0_ref, part_ref,
                     wwide_ref, cext_ref, cshift_ref, cwb_ref, conv_ref, pext_ref, gates_ref, *, tm):
    i = pl.program_id(1)
    wlat_ref = win_ref

    @pl.when(jnp.logical_and(pl.program_id(0) == 0, i == 0))
    def _():
        for c in range(0, SEG_GATES[1], WIDE_COPY_COLS):
            wwide_ref[:, c:c + WIDE_COPY_COLS] = win_ref[:, WIDE_START + c:WIDE_START + c + WIDE_COPY_COLS]
    hn = [_rms(x_ref[0, r:r + MERGE_ROWS, :], npre_ref[...]).astype(BF16) for r in range(0, tm, MERGE_ROWS)]

    def proj(w_ref, lo, hi):
        return jnp.concatenate([_dot(g, w_ref[:, lo:hi]) for g in hn], axis=0)

    def rows_dot(a, w):
        return jnp.concatenate([_dot(a[r:r + MERGE_ROWS], w) for r in range(0, tm, MERGE_ROWS)], axis=0)

    @pl.when(i == 0)
    def _():
        cext_ref[0:CONV_HALO, :] = jnp.zeros((CONV_HALO, CONV_CH), F32)
        pext_ref[0:POOL_HALO, :] = jnp.zeros((POOL_HALO, POOL_CH), F32)
        for j in range(CONV_W):
            cwb_ref[j] = jnp.broadcast_to(cw_ref[j:j + 1, :], (SUBLANES, CONV_CH))

    cext_ref[CONV_HALO:CONV_HALO + tm, :] = proj(wwide_ref, *SEG_CA) * jax.nn.sigmoid(proj(wwide_ref, *SEG_CG))
    ext_rows = tm + CONV_HALO
    ext = cext_ref[...]
    for r in range(1, SUBLANES):
        cshift_ref[r - 1] = pltpu.roll(ext, ext_rows - r, 0)[0:ext_rows - SUBLANES]
    first_shift = CONV_HALO - (CONV_W - 1)
    n_conv_steps = tm // CONV_ROWS
    gate_slabs = (SEG_GATES[1] - SEG_GATES[0]) // MXU_DIM
    slab = 0
    for step in range(n_conv_steps):
        r0 = step * CONV_ROWS
        accs = [None] * (CONV_ROWS // SUBLANES)
        for j in range(CONV_W):
            s = first_shift + j
            r, base = s % SUBLANES, r0 + s - s % SUBLANES
            w_tile = cwb_ref[j]
            for g in range(len(accs)):
                rows = slice(base + g * SUBLANES, base + (g + 1) * SUBLANES)
                src = cext_ref[rows, :] if r == 0 else cshift_ref[r - 1, rows, :]
                accs[g] = src * w_tile if accs[g] is None else accs[g] + src * w_tile
        conv_ref[r0:r0 + CONV_ROWS, :] = jnp.concatenate(accs, axis=0)
        slab_end = (step + 1) * gate_slabs // n_conv_steps
        while slab < slab_end:
            lo = slab * MXU_DIM
            gates_ref[:, lo:lo + MXU_DIM] = jax.nn.sigmoid(
                proj(wwide_ref, SEG_GATES[0] + lo, SEG_GATES[0] + lo + MXU_DIM))
            slab += 1
    cext_ref[0:CONV_HALO, :] = cext_ref[tm:tm + CONV_HALO, :]

    cos = cos_ref[0]
    sin = sin_ref[0]

    cq = _rms(proj(wlat_ref, *SEG_Q), qn_ref[...]).astype(BF16)
    qt = _dot(wuqt_ref[...], cq, _NT)
    for h in range(N_HEADS):
        base = h * HEAD_PAD
        r1, r2 = _rope_rows(qt[base + NOPE:base + NOPE + ROPE_HALF], qt[base + NOPE + ROPE_HALF:base + QK_DIM],
                            cos, sin)
        blk = jnp.concatenate([qt[base:base + NOPE], r1, r2, qt[base + QK_DIM:base + HEAD_PAD]], axis=0)
        qt_ref[0, base:base + HEAD_PAD, :] = (blk * Q_SCALE).astype(BF16)

    ckv = _rms(proj(wlat_ref, *SEG_KV), kvn_ref[...]).astype(BF16)
    kpad = rows_dot(ckv, wuk_ref[...])
    row_id = lax.broadcasted_iota(jnp.int32, (N_HEADS * V_ROWS, 1), 0)
    ones_row = sum((row_id == h * V_ROWS + V_DIM).astype(F32) for h in range(N_HEADS))
    vt_ref[0] = (_dot(wuvt_ref[...], ckv, _NT) + ones_row).astype(BF16)
    krt = proj(win_ref, LAT_COLS, LAT_COLS + HEAD_PAD).T
    r1, r2 = _rope_rows(krt[0:ROPE_HALF], krt[ROPE_HALF:ROPE], cos, sin)
    kr = jnp.concatenate([jnp.zeros((NOPE, tm), F32), r1, r2, jnp.zeros((HEAD_PAD - QK_DIM, tm), F32)], axis=0).T
    for h in range(N_HEADS):
        base = h * HEAD_PAD
        k_ref[0, :, base:base + HEAD_PAD] = (kpad[:, base:base + HEAD_PAD] + kr).astype(BF16)

    hc = conv_ref[...] + cb_ref[...]
    xc = hc - jnp.mean(hc, axis=-1, keepdims=True)
    ln = xc * lax.rsqrt(jnp.mean(xc * xc, axis=-1, keepdims=True) + EPS) * lng_ref[...] + lnb_ref[...]
    yconv = rows_dot((ln * jax.nn.sigmoid(ln)).astype(BF16), wco_ref[...])

    u = proj(wwide_ref, *SEG_POOL)
    pext_ref[POOL_HALO:POOL_HALO + tm, :] = u
    t1 = i * tm + 1 + lax.broadcasted_iota(jnp.int32, (tm, 1), 0)
    pool_rows = tm + POOL_HALO
    run = pext_ref[...]
    diffs = []
    for gi, w in enumerate(POOL_WINDOWS):
        run = run + pltpu.roll(run, w // 2, 0)
        tot = run[POOL_HALO:pool_rows, 0:POOL_GD]
        cnt = jnp.minimum(t1, w).astype(F32)
        diffs.append(tot / cnt - u[:, gi * POOL_GD:(gi + 1) * POOL_GD])
        run = run[:, POOL_GD:]
    pext_ref[0:POOL_HALO, :] = pext_ref[tm:tm + POOL_HALO, :]
    dmix = jnp.concatenate(diffs, axis=1).astype(BF16)
    mixed = rows_dot(dmix, pbd_ref[...]) * ps_ref[...]
    ypool = rows_dot(mixed.astype(BF16), wpo_ref[...])

    part = gates_ref[:, D_MODEL:2 * D_MODEL] * yconv + gates_ref[:, 2 * D_MODEL:3 * D_MODEL] * ypool
    part_ref[0] = part.astype(BF16)
    g0_ref[0] = gates_ref[:, 0:D_MODEL].astype(BF16)


def _layer_spec(stacked, l):
    zeros = (0,) * (stacked.ndim - 1)
    return pl.BlockSpec((None,) + stacked.shape[1:], lambda b, i: (l,) + zeros, pipeline_mode=pl.Buffered(1))


def _mixer_in(x, cos_t, sin_t, p, l, *, tm):
    b, s, d = x.shape
    nt = s // tm
    row_tile = lambda w: pl.BlockSpec((1, tm, w), lambda bi, i: (bi, i, 0))
    col_tile = lambda r: pl.BlockSpec((1, r, tm), lambda bi, i: (bi, 0, i))
    assert s % tm == 0 and tm % MERGE_ROWS == 0 and tm % CONV_ROWS == 0
    consts = [p[name] for name in (
        "npre", "w_in", "q_norm", "w_uq_t", "kv_norm", "w_uk", "w_uv_t", "conv_w", "conv_b",
        "ln_g", "ln_b", "w_conv_o", "pool_bd", "pool_scale", "w_pool_o")]
    return pl.pallas_call(
        functools.partial(_mixer_in_kernel, tm=tm),
        grid=(b, nt),
        in_specs=[row_tile(d), col_tile(ROPE_HALF), col_tile(ROPE_HALF)] + [_layer_spec(c, l) for c in consts],
        out_specs=[col_tile(N_HEADS * HEAD_PAD), row_tile(N_HEADS * HEAD_PAD), col_tile(N_HEADS * V_ROWS),
                   row_tile(d), row_tile(d)],
        out_shape=[
            jax.ShapeDtypeStruct((b, N_HEADS * HEAD_PAD, s), BF16),
            jax.ShapeDtypeStruct((b, s, N_HEADS * HEAD_PAD), BF16),
            jax.ShapeDtypeStruct((b, N_HEADS * V_ROWS, s), BF16),
            jax.ShapeDtypeStruct((b, s, d), BF16),
            jax.ShapeDtypeStruct((b, s, d), BF16),
        ],
        scratch_shapes=[
            pltpu.VMEM((d, SEG_GATES[1]), BF16),
            pltpu.VMEM((tm + CONV_HALO, CONV_CH), F32),
            pltpu.VMEM((SUBLANES - 1, tm + CONV_HALO - SUBLANES, CONV_CH), F32),
            pltpu.VMEM((CONV_W, SUBLANES, CONV_CH), F32),
            pltpu.VMEM((tm, CONV_CH), F32),
            pltpu.VMEM((tm + POOL_HALO, POOL_CH), F32),
            pltpu.VMEM((tm, N_BRANCHES * D_MODEL), F32),
        ],
        compiler_params=pltpu.CompilerParams(
            dimension_semantics=("arbitrary", "arbitrary"), vmem_limit_bytes=VMEM_LIMIT_BYTES),
        name="mixer_in",
    )(x, cos_t, sin_t, *consts)


def _attn_kernel(qt_ref, k_ref, vt_ref, ot_ref, s0_ref, s1_ref, m_ref, acc_ref, *, tq):
    i = pl.program_id(1)
    half = tq // 2
    m_ref[...] = jnp.full(m_ref.shape, -jnp.inf, F32)
    acc_ref[...] = jnp.zeros(acc_ref.shape, F32)

    def head(h):
        return slice(h * HEAD_PAD, (h + 1) * HEAD_PAD)

    def scores_into(s_ref, c):
        k0 = pl.multiple_of(c * tq, tq)
        for h in range(N_HEADS):
            s_ref[h] = _dot(k_ref[0, pl.ds(k0, tq), head(h)], qt_ref[0, head(h), :])

    def update(s_ref, c, k_lo, n_keys, q_lo, masked):
        k0 = pl.multiple_of(c * tq + k_lo, half)
        for h in range(N_HEADS):
            s = s_ref[h, k_lo:k_lo + n_keys, q_lo:tq]
            if masked:
                key = lax.broadcasted_iota(jnp.int32, (n_keys, 1), 0) + (k_lo - q_lo)
                qry = lax.broadcasted_iota(jnp.int32, (1, tq - q_lo), 1)
                s = jnp.where(key <= qry, s, -jnp.inf)
            m = m_ref[h, :, q_lo:tq]
            m_new = jnp.maximum(m, jnp.max(s, axis=0, keepdims=True))
            alpha = jnp.exp2(m - m_new)
            prob = jnp.exp2(s - m_new).astype(BF16)
            m_ref[h, :, q_lo:tq] = m_new
            vc = vt_ref[0, h * V_ROWS:(h + 1) * V_ROWS, pl.ds(k0, n_keys)]
            acc_ref[h, :, q_lo:tq] = alpha * acc_ref[h, :, q_lo:tq] + _dot(vc, prob)

    def diagonal(s_ref):
        update(s_ref, i, 0, half, 0, True)
        update(s_ref, i, half, half, half, True)

    def pair(p, carry):
        scores_into(s1_ref, 2 * p + 1)
        update(s0_ref, 2 * p, 0, tq, 0, False)
        scores_into(s0_ref, 2 * p + 2)
        update(s1_ref, 2 * p + 1, 0, tq, 0, False)
        return carry

    def tail_even():
        diagonal(s0_ref)

    def tail_odd():
        scores_into(s1_ref, i)
        update(s0_ref, i - 1, 0, tq, 0, False)
        diagonal(s1_ref)

    @pl.when(i == 0)
    def _():
        scores_into(s0_ref, 0)
        tail_even()

    @pl.when(i == 1)
    def _():
        scores_into(s0_ref, 0)
        tail_odd()

    @pl.when(i >= 2)
    def _():
        scores_into(s0_ref, 0)
        pair(0, 0)

    lax.fori_loop(1, i // 2, pair, 0)

    @pl.when(jnp.logical_and(i >= 2, i % 2 == 0))
    def _():
        tail_even()

    @pl.when(jnp.logical_and(i >= 2, i % 2 == 1))
    def _():
        tail_odd()

    for h in range(N_HEADS):
        ot_ref[0, h * V_DIM:(h + 1) * V_DIM, :] = (acc_ref[h, 0:V_DIM, :] / acc_ref[h, V_DIM:V_DIM + 1, :]).astype(BF16)


def _attention(q_t, k, v_t, *, tq):
    b, _, s = q_t.shape
    tk = tq
    assert s % tq == 0
    return pl.pallas_call(
        functools.partial(_attn_kernel, tq=tq),
        grid=(b, s // tq),
        in_specs=[
            pl.BlockSpec((1, N_HEADS * HEAD_PAD, tq), lambda bi, i: (bi, 0, i)),
            pl.BlockSpec((1, s, N_HEADS * HEAD_PAD), lambda bi, i: (bi, 0, 0)),
            pl.BlockSpec((1, N_HEADS * V_ROWS, s), lambda bi, i: (bi, 0, 0)),
        ],
        out_specs=pl.BlockSpec((1, N_HEADS * V_DIM, tq), lambda bi, i: (bi, 0, i)),
        out_shape=jax.ShapeDtypeStruct((b, N_HEADS * V_DIM, s), BF16),
        scratch_shapes=[
            pltpu.VMEM((N_HEADS, tk, tq), F32),
            pltpu.VMEM((N_HEADS, tk, tq), F32),
            pltpu.VMEM((N_HEADS, 1, tq), F32),
            pltpu.VMEM((N_HEADS, V_ROWS, tq), F32),
        ],
        compiler_params=pltpu.CompilerParams(
            dimension_semantics=("arbitrary", "arbitrary"), vmem_limit_bytes=VMEM_LIMIT_BYTES),
        name="attention",
    )(q_t, k, v_t)


def _merge_ffn_kernel(h_ref, ot_ref, g0_ref, part_ref, wao_ref, wmo_ref, npost_ref, fpre_ref, wg_ref, wu_ref,
                      wd_ref, fpost_ref, out_ref):
    tm = h_ref.shape[1]
    groups = [slice(r, r + MERGE_ROWS) for r in range(0, tm, MERGE_ROWS)]
    h1 = []
    for rows in groups:
        yattn = _dot(ot_ref[0, :, rows], wao_ref[...], _TN)
        merged = g0_ref[0, rows, :].astype(F32) * yattn + part_ref[0, rows, :].astype(F32)
        h1.append(h_ref[0, rows, :] + _rms(_dot(merged.astype(BF16), wmo_ref[...]), npost_ref[...]))
    act = []
    for g in range(len(groups)):
        hn = _rms(h1[g], fpre_ref[...]).astype(BF16)
        gate = _dot(hn, wg_ref[...])
        act.append((gate * jax.nn.sigmoid(gate) * _dot(hn, wu_ref[...])).astype(BF16))
    for g, rows in enumerate(groups):
        out_ref[0, rows, :] = h1[g] + _rms(_dot(act[g], wd_ref[...]), fpost_ref[...])


def _merge_ffn(h, o_t, g0, part, p, l, *, tm):
    b, s, d = h.shape
    row_tile = lambda w: pl.BlockSpec((1, tm, w), lambda bi, i: (bi, i, 0))
    assert s % tm == 0 and tm % MERGE_ROWS == 0
    consts = [p["w_attn_o"], p["w_mix_o"], p["npost"], p["fpre"], p["w_gate"], p["w_up"], p["w_down"], p["fpost"]]
    return pl.pallas_call(
        _merge_ffn_kernel,
        grid=(b, s // tm),
        in_specs=[row_tile(d), pl.BlockSpec((1, N_HEADS * V_DIM, tm), lambda bi, i: (bi, 0, i)), row_tile(d),
                  row_tile(d)] + [_layer_spec(c, l) for c in consts],
        out_specs=row_tile(d),
        out_shape=jax.ShapeDtypeStruct((b, s, d), F32),
        compiler_params=pltpu.CompilerParams(
            dimension_semantics=("arbitrary", "arbitrary"), vmem_limit_bytes=VMEM_LIMIT_BYTES),
        name="merge_ffn",
    )(h, o_t, g0, part, *consts)


def _stacked_params(mix_norm_pre, w_in, q_norm, w_uq, kv_norm, w_uk, w_uv, w_attn_o, conv_w, conv_b, conv_ln_g,
                    conv_ln_b, w_conv_o, pool_w, pool_scale, w_pool_o, w_mix_o, mix_norm_post, ffn_norm_pre, w_gate,
                    w_up, w_down, ffn_norm_post):
    depth = w_in.shape[0]
    row = lambda v: v.reshape(depth, 1, -1)
    assert w_in.shape[2] - WIDE_START == SEG_GATES[1] and SEG_GATES[1] % WIDE_COPY_COLS == 0
    w_in_p = jnp.pad(w_in.astype(BF16), ((0, 0), (0, 0), (0, -w_in.shape[2] % LANES)))

    def pad_heads(w, dh, dh_pad):
        w = jnp.pad(w.reshape(depth, w.shape[1], N_HEADS, dh), ((0, 0), (0, 0), (0, 0), (0, dh_pad - dh)))
        return w.reshape(depth, w.shape[1], N_HEADS * dh_pad)

    transpose = lambda w: jnp.swapaxes(w, 1, 2)
    pool_bd = jnp.zeros((depth, POOL_CH, POOL_CH), F32)
    for gi in range(len(POOL_WINDOWS)):
        sl = slice(gi * POOL_GD, (gi + 1) * POOL_GD)
        pool_bd = pool_bd.at[:, sl, sl].set(pool_w[:, gi])
    return dict(
        npre=row(mix_norm_pre), w_in=w_in_p, q_norm=row(q_norm),
        w_uq_t=transpose(pad_heads(w_uq, QK_DIM, HEAD_PAD)).astype(BF16), kv_norm=row(kv_norm),
        w_uk=pad_heads(w_uk, NOPE, HEAD_PAD).astype(BF16),
        w_uv_t=transpose(pad_heads(w_uv, V_DIM, V_ROWS)).astype(BF16),
        conv_w=conv_w, conv_b=row(conv_b), ln_g=row(conv_ln_g), ln_b=row(conv_ln_b),
        w_conv_o=w_conv_o.astype(BF16), pool_bd=pool_bd.astype(BF16), pool_scale=row(pool_scale),
        w_pool_o=w_pool_o.astype(BF16),
        w_attn_o=w_attn_o.astype(BF16), w_mix_o=w_mix_o.astype(BF16), npost=row(mix_norm_post),
        fpre=row(ffn_norm_pre), w_gate=w_gate.astype(BF16), w_up=w_up.astype(BF16),
        w_down=w_down.astype(BF16), fpost=row(ffn_norm_post),
    )


def kernel(x, positions, mix_norm_pre, w_in, q_norm, w_uq, kv_norm, w_uk, w_uv, w_attn_o, conv_w, conv_b, conv_ln_g, conv_ln_b, w_conv_o, pool_w, pool_scale, w_pool_o, w_mix_o, mix_norm_post, ffn_norm_pre, w_gate, w_up, w_down, ffn_norm_post):
    p = _stacked_params(mix_norm_pre, w_in, q_norm, w_uq, kv_norm, w_uk, w_uv, w_attn_o, conv_w, conv_b, conv_ln_g,
                        conv_ln_b, w_conv_o, pool_w, pool_scale, w_pool_o, w_mix_o, mix_norm_post, ffn_norm_pre,
                        w_gate, w_up, w_down, ffn_norm_post)
    assert x.dtype == F32 and x.shape[2] == D_MODEL and positions.shape == x.shape[:2]
    s = x.shape[1]
    tm = min(ROW_TILE, s)
    cos_t, sin_t = _rope_tables(positions)
    h = x
    for l in range(w_in.shape[0]):
        q_t, k, v_t, g0, part = _mixer_in(h, cos_t, sin_t, p, l, tm=tm)
        o_t = _attention(q_t, k, v_t, tq=min(ATTN_TQ, s))
        h = _merge_ffn(h, o_t, g0, part, p, l, tm=tm)
    return h
```

```python
import functools
import math

import jax
import jax.numpy as jnp
from jax import lax
from jax.experimental import pallas as pl
from jax.experimental.pallas import tpu as pltpu

D_MODEL = 1024
N_HEADS = 8
NOPE = 64
ROPE = 32
ROPE_HALF = ROPE // 2
V_DIM = 64
Q_RANK = 384
KV_RANK = 256
ROPE_THETA = 10000.0
CONV_CH = 512
CONV_W = 31
POOL_WINDOWS = (2, 4, 8, 16)
POOL_CH = 512
POOL_GD = POOL_CH // len(POOL_WINDOWS)
N_BRANCHES = 3
D_FF = 2816
EPS = 1e-6

LANES = 128
SUBLANES = 8
BF16_ROWS = 16
MXU_DIM = 256
VMEM_LIMIT_BYTES = 56 * 1024 * 1024

HEAD_PAD = LANES
QK_DIM = NOPE + ROPE
V_ROWS = V_DIM + BF16_ROWS
Q_SCALE = (1.0 / math.sqrt(QK_DIM)) * math.log2(math.e)

LAT_COLS = Q_RANK + KV_RANK
WIDE_START = LAT_COLS + ROPE
SEG_Q = (0, Q_RANK)
SEG_KV = (Q_RANK, LAT_COLS)
SEG_CA = (0, CONV_CH)
SEG_CG = (CONV_CH, 2 * CONV_CH)
SEG_POOL = (2 * CONV_CH, 2 * CONV_CH + POOL_CH)
SEG_GATES = (SEG_POOL[1], SEG_POOL[1] + N_BRANCHES * D_MODEL)

CONV_HALO = 32
POOL_HALO = 16
CONV_ROWS = 32

ROW_TILE = 512
MERGE_ROWS = 256
ATTN_TQ = 512

F32 = jnp.float32
BF16 = jnp.bfloat16

_NN = (((1,), (0,)), ((), ()))
_NT = (((1,), (1,)), ((), ()))
_TN = (((0,), (0,)), ((), ()))


def _rms(x, g):
    return x * lax.rsqrt(jnp.mean(x * x, axis=-1, keepdims=True) + EPS) * g


def _dot(a, b, dims=_NN):
    return lax.dot_general(a, b, dims, preferred_element_type=F32)


def _rope_rows(x1, x2, cos, sin):
    return x1 * cos - x2 * sin, x2 * cos + x1 * sin


def _rope_table_kernel(pos_ref, invf_ref, cos_ref, sin_ref):
    ang = invf_ref[...] * pos_ref[0].astype(F32)
    cos_ref[0] = jnp.cos(ang)
    sin_ref[0] = jnp.sin(ang)


def _rope_tables(positions):
    b, s = positions.shape
    inv_freq = ROPE_THETA ** (-jnp.arange(0, ROPE, 2, dtype=F32) / ROPE)
    out = jax.ShapeDtypeStruct((b, ROPE_HALF, s), F32)
    return pl.pallas_call(
        _rope_table_kernel,
        grid=(b,),
        in_specs=[
            pl.BlockSpec((1, 1, s), lambda i: (i, 0, 0)),
            pl.BlockSpec((ROPE_HALF, 1), lambda i: (0, 0)),
        ],
        out_specs=[
            pl.BlockSpec((1, ROPE_HALF, s), lambda i: (i, 0, 0)),
            pl.BlockSpec((1, ROPE_HALF, s), lambda i: (i, 0, 0)),
        ],
        out_shape=[out, out],
        name="rope_tables",
    )(positions.reshape(b, 1, s), inv_freq.reshape(ROPE_HALF, 1))


def _mixer_in_kernel(x_ref, cos_ref, sin_ref, npre_ref, wlat_ref, wkr_ref, wwide_ref, qn_ref, wuqt_ref, kvn_ref,
                     wuk_ref, wuvt_ref, cw_ref, cb_ref, lng_ref, lnb_ref, wco_ref, pw_ref, ps_ref, wpo_ref,
                     qt_ref, k_ref, vt_ref, g0_ref, part_ref,
                     cext_ref, cshift_ref, cwb_ref, conv_ref, pext_ref, gates_ref, *, tm):
    i = pl.program_id(1)
    hn = [_rms(x_ref[0, r:r + MERGE_ROWS, :], npre_ref[...]).astype(BF16) for r in range(0, tm, MERGE_ROWS)]

    def proj(w_ref, lo, hi):
        return jnp.concatenate([_dot(g, w_ref[:, lo:hi]) for g in hn], axis=0)

    def rows_dot(a, w):
        return jnp.concatenate([_dot(a[r:r + MERGE_ROWS], w) for r in range(0, tm, MERGE_ROWS)], axis=0)

    @pl.when(i == 0)
    def _():
        cext_ref[0:CONV_HALO, :] = jnp.zeros((CONV_HALO, CONV_CH), F32)
        pext_ref[0:POOL_HALO, :] = jnp.zeros((POOL_HALO, POOL_CH), F32)
        for j in range(CONV_W):
            cwb_ref[j] = jnp.broadcast_to(cw_ref[j:j + 1, :], (SUBLANES, CONV_CH))

    cext_ref[CONV_HALO:CONV_HALO + tm, :] = proj(wwide_ref, *SEG_CA) * jax.nn.sigmoid(proj(wwide_ref, *SEG_CG))
    ext_rows = tm + CONV_HALO
    ext = cext_ref[...]
    for r in range(1, SUBLANES):
        cshift_ref[r - 1] = pltpu.roll(ext, ext_rows - r, 0)[0:ext_rows - SUBLANES]
    first_shift = CONV_HALO - (CONV_W - 1)
    n_conv_steps = tm // CONV_ROWS
    gate_slabs = (SEG_GATES[1] - SEG_GATES[0]) // MXU_DIM
    slab = 0
    for step in range(n_conv_steps):
        r0 = step * CONV_ROWS
        accs = [None] * (CONV_ROWS // SUBLANES)
        for j in range(CONV_W):
            s = first_shift + j
            r, base = s % SUBLANES, r0 + s - s % SUBLANES
            w_tile = cwb_ref[j]
            for g in range(len(accs)):
                rows = slice(base + g * SUBLANES, base + (g + 1) * SUBLANES)
                src = cext_ref[rows, :] if r == 0 else cshift_ref[r - 1, rows, :]
                accs[g] = src * w_tile if accs[g] is None else accs[g] + src * w_tile
        conv_ref[r0:r0 + CONV_ROWS, :] = jnp.concatenate(accs, axis=0)
        slab_end = (step + 1) * gate_slabs // n_conv_steps
        while slab < slab_end:
            lo = slab * MXU_DIM
            gates_ref[:, lo:lo + MXU_DIM] = jax.nn.sigmoid(
                proj(wwide_ref, SEG_GATES[0] + lo, SEG_GATES[0] + lo + MXU_DIM))
            slab += 1
    cext_ref[0:CONV_HALO, :] = cext_ref[tm:tm + CONV_HALO, :]

    cos = cos_ref[0]
    sin = sin_ref[0]

    cq = _rms(proj(wlat_ref, *SEG_Q), qn_ref[...]).astype(BF16)
    qt = _dot(wuqt_ref[...], cq, _NT)
    for h in range(N_HEADS):
        base = h * HEAD_PAD
        r1, r2 = _rope_rows(qt[base + NOPE:base + NOPE + ROPE_HALF], qt[base + NOPE + ROPE_HALF:base + QK_DIM],
                            cos, sin)
        blk = jnp.concatenate([qt[base:base + NOPE], r1, r2, qt[base + QK_DIM:base + HEAD_PAD]], axis=0)
        qt_ref[0, base:base + HEAD_PAD, :] = (blk * Q_SCALE).astype(BF16)

    ckv = _rms(proj(wlat_ref, *SEG_KV), kvn_ref[...]).astype(BF16)
    kpad = rows_dot(ckv, wuk_ref[...])
    row_id = lax.broadcasted_iota(jnp.int32, (N_HEADS * V_ROWS, 1), 0)
    ones_row = sum((row_id == h * V_ROWS + V_DIM).astype(F32) for h in range(N_HEADS))
    vt_ref[0] = (_dot(wuvt_ref[...], ckv, _NT) + ones_row).astype(BF16)
    krt = proj(wkr_ref, 0, HEAD_PAD).T
    r1, r2 = _rope_rows(krt[0:ROPE_HALF], krt[ROPE_HALF:ROPE], cos, sin)
    kr = jnp.concatenate([jnp.zeros((NOPE, tm), F32), r1, r2, jnp.zeros((HEAD_PAD - QK_DIM, tm), F32)], axis=0).T
    for h in range(N_HEADS):
        base = h * HEAD_PAD
        k_ref[0, :, base:base + HEAD_PAD] = (kpad[:, base:base + HEAD_PAD] + kr).astype(BF16)

    hc = conv_ref[...] + cb_ref[...]
    xc = hc - jnp.mean(hc, axis=-1, keepdims=True)
    ln = xc * lax.rsqrt(jnp.mean(xc * xc, axis=-1, keepdims=True) + EPS) * lng_ref[...] + lnb_ref[...]
    yconv = rows_dot((ln * jax.nn.sigmoid(ln)).astype(BF16), wco_ref[...])

    u = proj(wwide_ref, *SEG_POOL)
    pext_ref[POOL_HALO:POOL_HALO + tm, :] = u
    t1 = i * tm + 1 + lax.broadcasted_iota(jnp.int32, (tm, 1), 0)
    pool_rows = tm + POOL_HALO
    run = pext_ref[...]
    diffs = []
    for gi, w in enumerate(POOL_WINDOWS):
        run = run + pltpu.roll(run, w // 2, 0)
        tot = run[POOL_HALO:pool_rows, 0:POOL_GD]
        cnt = jnp.minimum(t1, w).astype(F32)
        diffs.append(tot / cnt - u[:, gi * POOL_GD:(gi + 1) * POOL_GD])
        run = run[:, POOL_GD:]
    pext_ref[0:POOL_HALO, :] = pext_ref[tm:tm + POOL_HALO, :]
    dmix = jnp.concatenate(diffs, axis=1).astype(BF16)
    mixed = jnp.concatenate([rows_dot(dmix[:, gi * POOL_GD:(gi + 1) * POOL_GD], pw_ref[gi])
                             for gi in range(len(POOL_WINDOWS))], axis=1) * ps_ref[...]
    ypool = rows_dot(mixed.astype(BF16), wpo_ref[...])

    part = gates_ref[:, D_MODEL:2 * D_MODEL] * yconv + gates_ref[:, 2 * D_MODEL:3 * D_MODEL] * ypool
    part_ref[0] = part.astype(BF16)
    g0_ref[0] = gates_ref[:, 0:D_MODEL].astype(BF16)


def _layer_spec(stacked, l):
    zeros = (0,) * (stacked.ndim - 1)
    return pl.BlockSpec((None,) + stacked.shape[1:], lambda b, i: (l,) + zeros, pipeline_mode=pl.Buffered(1))


def _mixer_in(x, cos_t, sin_t, p, l, *, tm):
    b, s, d = x.shape
    nt = s // tm
    row_tile = lambda w: pl.BlockSpec((1, tm, w), lambda bi, i: (bi, i, 0))
    col_tile = lambda r: pl.BlockSpec((1, r, tm), lambda bi, i: (bi, 0, i))
    assert s % tm == 0 and tm % MERGE_ROWS == 0 and tm % CONV_ROWS == 0
    consts = [p[name] for name in (
        "npre", "w_lat", "w_kr", "w_wide", "q_norm", "w_uq_t", "kv_norm", "w_uk", "w_uv_t", "conv_w", "conv_b",
        "ln_g", "ln_b", "w_conv_o", "pool_w", "pool_scale", "w_pool_o")]
    return pl.pallas_call(
        functools.partial(_mixer_in_kernel, tm=tm),
        grid=(b, nt),
        in_specs=[row_tile(d), col_tile(ROPE_HALF), col_tile(ROPE_HALF)] + [_layer_spec(c, l) for c in consts],
        out_specs=[col_tile(N_HEADS * HEAD_PAD), row_tile(N_HEADS * HEAD_PAD), col_tile(N_HEADS * V_ROWS),
                   row_tile(d), row_tile(d)],
        out_shape=[
            jax.ShapeDtypeStruct((b, N_HEADS * HEAD_PAD, s), BF16),
            jax.ShapeDtypeStruct((b, s, N_HEADS * HEAD_PAD), BF16),
            jax.ShapeDtypeStruct((b, N_HEADS * V_ROWS, s), BF16),
            jax.ShapeDtypeStruct((b, s, d), BF16),
            jax.ShapeDtypeStruct((b, s, d), BF16),
        ],
        scratch_shapes=[
            pltpu.VMEM((tm + CONV_HALO, CONV_CH), F32),
            pltpu.VMEM((SUBLANES - 1, tm + CONV_HALO - SUBLANES, CONV_CH), F32),
            pltpu.VMEM((CONV_W, SUBLANES, CONV_CH), F32),
            pltpu.VMEM((tm, CONV_CH), F32),
            pltpu.VMEM((tm + POOL_HALO, POOL_CH), F32),
            pltpu.VMEM((tm, N_BRANCHES * D_MODEL), F32),
        ],
        compiler_params=pltpu.CompilerParams(
            dimension_semantics=("arbitrary", "arbitrary"), vmem_limit_bytes=VMEM_LIMIT_BYTES),
        name="mixer_in",
    )(x, cos_t, sin_t, *consts)


def _attn_kernel(qt_ref, k_ref, vt_ref, ot_ref, s0_ref, s1_ref, m_ref, acc_ref, *, tq):
    i = pl.program_id(1)
    half = tq // 2
    m_ref[...] = jnp.full(m_ref.shape, -jnp.inf, F32)
    acc_ref[...] = jnp.zeros(acc_ref.shape, F32)

    def head(h):
        return slice(h * HEAD_PAD, (h + 1) * HEAD_PAD)

    def scores_into(s_ref, c):
        k0 = pl.multiple_of(c * tq, tq)
        for h in range(N_HEADS):
            s_ref[h] = _dot(k_ref[0, pl.ds(k0, tq), head(h)], qt_ref[0, head(h), :])

    def update(s_ref, c, k_lo, n_keys, q_lo, masked):
        k0 = pl.multiple_of(c * tq + k_lo, half)
        for h in range(N_HEADS):
            s = s_ref[h, k_lo:k_lo + n_keys, q_lo:tq]
            if masked:
                key = lax.broadcasted_iota(jnp.int32, (n_keys, 1), 0) + (k_lo - q_lo)
                qry = lax.broadcasted_iota(jnp.int32, (1, tq - q_lo), 1)
                s = jnp.where(key <= qry, s, -jnp.inf)
            m = m_ref[h, :, q_lo:tq]
            m_new = jnp.maximum(m, jnp.max(s, axis=0, keepdims=True))
            alpha = jnp.exp2(m - m_new)
            prob = jnp.exp2(s - m_new).astype(BF16)
            m_ref[h, :, q_lo:tq] = m_new
            vc = vt_ref[0, h * V_ROWS:(h + 1) * V_ROWS, pl.ds(k0, n_keys)]
            acc_ref[h, :, q_lo:tq] = alpha * acc_ref[h, :, q_lo:tq] + _dot(vc, prob)

    def diagonal(s_ref):
        update(s_ref, i, 0, half, 0, True)
        update(s_ref, i, half, half, half, True)

    def pair(p, carry):
        scores_into(s1_ref, 2 * p + 1)
        update(s0_ref, 2 * p, 0, tq, 0, False)
        scores_into(s0_ref, 2 * p + 2)
        update(s1_ref, 2 * p + 1, 0, tq, 0, False)
        return carry

    def tail_even():
        diagonal(s0_ref)

    def tail_odd():
        scores_into(s1_ref, i)
        update(s0_ref, i - 1, 0, tq, 0, False)
        diagonal(s1_ref)

    @pl.when(i == 0)
    def _():
        scores_into(s0_ref, 0)
        tail_even()

    @pl.when(i == 1)
    def _():
        scores_into(s0_ref, 0)
        tail_odd()

    @pl.when(i >= 2)
    def _():
        scores_into(s0_ref, 0)
        pair(0, 0)

    lax.fori_loop(1, i // 2, pair, 0)

    @pl.when(jnp.logical_and(i >= 2, i % 2 == 0))
    def _():
        tail_even()

    @pl.when(jnp.logical_and(i >= 2, i % 2 == 1))
    def _():
        tail_odd()

    for h in range(N_HEADS):
        ot_ref[0, h * V_DIM:(h + 1) * V_DIM, :] = (acc_ref[h, 0:V_DIM, :] / acc_ref[h, V_DIM:V_DIM + 1, :]).astype(BF16)


def _attention(q_t, k, v_t, *, tq):
    b, _, s = q_t.shape
    tk = tq
    assert s % tq == 0
    return pl.pallas_call(
        functools.partial(_attn_kernel, tq=tq),
        grid=(b, s // tq),
        in_specs=[
            pl.BlockSpec((1, N_HEADS * HEAD_PAD, tq), lambda bi, i: (bi, 0, i)),
            pl.BlockSpec((1, s, N_HEADS * HEAD_PAD), lambda bi, i: (bi, 0, 0)),
            pl.BlockSpec((1, N_HEADS * V_ROWS, s), lambda bi, i: (bi, 0, 0)),
        ],
        out_specs=pl.BlockSpec((1, N_HEADS * V_DIM, tq), lambda bi, i: (bi, 0, i)),
        out_shape=jax.ShapeDtypeStruct((b, N_HEADS * V_DIM, s), BF16),
        scratch_shapes=[
            pltpu.VMEM((N_HEADS, tk, tq), F32),
            pltpu.VMEM((N_HEADS, tk, tq), F32),
            pltpu.VMEM((N_HEADS, 1, tq), F32),
            pltpu.VMEM((N_HEADS, V_ROWS, tq), F32),
        ],
        compiler_params=pltpu.CompilerParams(
            dimension_semantics=("arbitrary", "arbitrary"), vmem_limit_bytes=VMEM_LIMIT_BYTES),
        name="attention",
    )(q_t, k, v_t)


def _merge_ffn_kernel(h_ref, ot_ref, g0_ref, part_ref, wao_ref, wmo_ref, npost_ref, fpre_ref, wg_ref, wu_ref,
                      wd_ref, fpost_ref, out_ref):
    tm = h_ref.shape[1]
    groups = [slice(r, r + MERGE_ROWS) for r in range(0, tm, MERGE_ROWS)]
    h1 = []
    for rows in groups:
        yattn = _dot(ot_ref[0, :, rows], wao_ref[...], _TN)
        merged = g0_ref[0, rows, :].astype(F32) * yattn + part_ref[0, rows, :].astype(F32)
        h1.append(h_ref[0, rows, :] + _rms(_dot(merged.astype(BF16), wmo_ref[...]), npost_ref[...]))
    act = []
    for g in range(len(groups)):
        hn = _rms(h1[g], fpre_ref[...]).astype(BF16)
        gate = _dot(hn, wg_ref[...])
        act.append((gate * jax.nn.sigmoid(gate) * _dot(hn, wu_ref[...])).astype(BF16))
    for g, rows in enumerate(groups):
        out_ref[0, rows, :] = h1[g] + _rms(_dot(act[g], wd_ref[...]), fpost_ref[...])


def _merge_ffn(h, o_t, g0, part, p, l, *, tm):
    b, s, d = h.shape
    row_tile = lambda w: pl.BlockSpec((1, tm, w), lambda bi, i: (bi, i, 0))
    assert s % tm == 0 and tm % MERGE_ROWS == 0
    consts = [p["w_attn_o"], p["w_mix_o"], p["npost"], p["fpre"], p["w_gate"], p["w_up"], p["w_down"], p["fpost"]]
    return pl.pallas_call(
        _merge_ffn_kernel,
        grid=(b, s // tm),
        in_specs=[row_tile(d), pl.BlockSpec((1, N_HEADS * V_DIM, tm), lambda bi, i: (bi, 0, i)), row_tile(d),
                  row_tile(d)] + [_layer_spec(c, l) for c in consts],
        out_specs=row_tile(d),
        out_shape=jax.ShapeDtypeStruct((b, s, d), F32),
        compiler_params=pltpu.CompilerParams(
            dimension_semantics=("arbitrary", "arbitrary"), vmem_limit_bytes=VMEM_LIMIT_BYTES),
        name="merge_ffn",
    )(h, o_t, g0, part, *consts)


def _stacked_params(mix_norm_pre, w_in, q_norm, w_uq, kv_norm, w_uk, w_uv, w_attn_o, conv_w, conv_b, conv_ln_g,
                    conv_ln_b, w_conv_o, pool_w, pool_scale, w_pool_o, w_mix_o, mix_norm_post, ffn_norm_pre, w_gate,
                    w_up, w_down, ffn_norm_post):
    depth = w_in.shape[0]
    row = lambda v: v.reshape(depth, 1, -1)
    w_in = w_in.astype(BF16)
    w_kr = jnp.pad(w_in[:, :, LAT_COLS:WIDE_START], ((0, 0), (0, 0), (0, HEAD_PAD - ROPE)))
    assert w_in.shape[2] - WIDE_START == SEG_GATES[1]

    def pad_heads(w, dh, dh_pad):
        w = jnp.pad(w.reshape(depth, w.shape[1], N_HEADS, dh), ((0, 0), (0, 0), (0, 0), (0, dh_pad - dh)))
        return w.reshape(depth, w.shape[1], N_HEADS * dh_pad)

    transpose = lambda w: jnp.swapaxes(w, 1, 2)
    return dict(
        npre=row(mix_norm_pre), w_lat=w_in[:, :, :LAT_COLS], w_kr=w_kr,
        w_wide=w_in[:, :, WIDE_START:], q_norm=row(q_norm),
        w_uq_t=transpose(pad_heads(w_uq, QK_DIM, HEAD_PAD)).astype(BF16), kv_norm=row(kv_norm),
        w_uk=pad_heads(w_uk, NOPE, HEAD_PAD).astype(BF16),
        w_uv_t=transpose(pad_heads(w_uv, V_DIM, V_ROWS)).astype(BF16),
        conv_w=conv_w, conv_b=row(conv_b), ln_g=row(conv_ln_g), ln_b=row(conv_ln_b),
        w_conv_o=w_conv_o.astype(BF16), pool_w=pool_w.astype(BF16), pool_scale=row(pool_scale),
        w_pool_o=w_pool_o.astype(BF16),
        w_attn_o=w_attn_o.astype(BF16), w_mix_o=w_mix_o.astype(BF16), npost=row(mix_norm_post),
        fpre=row(ffn_norm_pre), w_gate=w_gate.astype(BF16), w_up=w_up.astype(BF16),
        w_down=w_down.astype(BF16), fpost=row(ffn_norm_post),
    )


def kernel(x, positions, mix_norm_pre, w_in, q_norm, w_uq, kv_norm, w_uk, w_uv, w_attn_o, conv_w, conv_b, conv_ln_g, conv_ln_b, w_conv_o, pool_w, pool_scale, w_pool_o, w_mix_o, mix_norm_post, ffn_norm_pre, w_gate, w_up, w_down, ffn_norm_post):
    p = _stacked_params(mix_norm_pre, w_in, q_norm, w_uq, kv_norm, w_uk, w_uv, w_attn_o, conv_w, conv_b, conv_ln_g,
                        conv_ln_b, w_conv_o, pool_w, pool_scale, w_pool_o, w_mix_o, mix_norm_post, ffn_norm_pre,
                        w_gate, w_up, w_down, ffn_norm_post)
    assert x.dtype == F32 and x.shape[2] == D_MODEL and positions.shape == x.shape[:2]
    s = x.shape[1]
    tm = min(ROW_TILE, s)
    cos_t, sin_t = _rope_tables(positions)
    h = x
    for l in range(w_in.shape[0]):
        q_t, k, v_t, g0, part = _mixer_in(h, cos_t, sin_t, p, l, tm=tm)
        o_t = _attention(q_t, k, v_t, tq=min(ATTN_TQ, s))
        h = _merge_ffn(h, o_t, g0, part, p, l, tm=tm)
    return h
```

```python
import functools
import math

import jax
import jax.numpy as jnp
from jax import lax
from jax.experimental import pallas as pl
from jax.experimental.pallas import tpu as pltpu

D_MODEL = 1024
N_HEADS = 8
NOPE = 64
ROPE = 32
ROPE_HALF = ROPE // 2
V_DIM = 64
Q_RANK = 384
KV_RANK = 256
ROPE_THETA = 10000.0
CONV_CH = 512
CONV_W = 31
POOL_WINDOWS = (2, 4, 8, 16)
POOL_CH = 512
POOL_GD = POOL_CH // len(POOL_WINDOWS)
N_BRANCHES = 3
D_FF = 2816
EPS = 1e-6

LANES = 128
SUBLANES = 8
BF16_ROWS = 16
MXU_DIM = 256
VMEM_LIMIT_BYTES = 56 * 1024 * 1024

HEAD_PAD = LANES
QK_DIM = NOPE + ROPE
V_ROWS = V_DIM + BF16_ROWS
Q_SCALE = (1.0 / math.sqrt(QK_DIM)) * math.log2(math.e)

LAT_COLS = Q_RANK + KV_RANK
WIDE_START = LAT_COLS + ROPE
SEG_Q = (0, Q_RANK)
SEG_KV = (Q_RANK, LAT_COLS)
SEG_CA = (0, CONV_CH)
SEG_CG = (CONV_CH, 2 * CONV_CH)
SEG_POOL = (2 * CONV_CH, 2 * CONV_CH + POOL_CH)
SEG_GATES = (SEG_POOL[1], SEG_POOL[1] + N_BRANCHES * D_MODEL)

CONV_HALO = 32
POOL_HALO = 16
CONV_ROWS = 32

ROW_TILE = 512
MERGE_ROWS = 256
ATTN_TQ = 512

F32 = jnp.float32
BF16 = jnp.bfloat16

_NN = (((1,), (0,)), ((), ()))
_NT = (((1,), (1,)), ((), ()))
_TN = (((0,), (0,)), ((), ()))


def _rms(x, g):
    return x * lax.rsqrt(jnp.mean(x * x, axis=-1, keepdims=True) + EPS) * g


def _dot(a, b, dims=_NN):
    return lax.dot_general(a, b, dims, preferred_element_type=F32)


def _rope_rows(x1, x2, cos, sin):
    return x1 * cos - x2 * sin, x2 * cos + x1 * sin


def _rope_table_kernel(pos_ref, invf_ref, cos_ref, sin_ref):
    ang = invf_ref[...] * pos_ref[0].astype(F32)
    cos_ref[0] = jnp.cos(ang)
    sin_ref[0] = jnp.sin(ang)


def _rope_tables(positions):
    b, s = positions.shape
    inv_freq = ROPE_THETA ** (-jnp.arange(0, ROPE, 2, dtype=F32) / ROPE)
    out = jax.ShapeDtypeStruct((b, ROPE_HALF, s), F32)
    return pl.pallas_call(
        _rope_table_kernel,
        grid=(b,),
        in_specs=[
            pl.BlockSpec((1, 1, s), lambda i: (i, 0, 0)),
            pl.BlockSpec((ROPE_HALF, 1), lambda i: (0, 0)),
        ],
        out_specs=[
            pl.BlockSpec((1, ROPE_HALF, s), lambda i: (i, 0, 0)),
            pl.BlockSpec((1, ROPE_HALF, s), lambda i: (i, 0, 0)),
        ],
        out_shape=[out, out],
        name="rope_tables",
    )(positions.reshape(b, 1, s), inv_freq.reshape(ROPE_HALF, 1))


def _mixer_in_kernel(x_ref, cos_ref, sin_ref, npre_ref, wlat_ref, wkr_ref, wwide_ref, qn_ref, wuqt_ref, kvn_ref,
                     wuk_ref, wuvt_ref, cw_ref, cb_ref, lng_ref, lnb_ref, wco_ref, pw_ref, ps_ref, wpo_ref,
                     qt_ref, k_ref, vt_ref, g0_ref, part_ref,
                     cext_ref, cshift_ref, cwb_ref, conv_ref, pext_ref, gates_ref, *, tm):
    i = pl.program_id(1)
    hn = [_rms(x_ref[0, r:r + MERGE_ROWS, :], npre_ref[...]).astype(BF16) for r in range(0, tm, MERGE_ROWS)]

    def proj(w_ref, lo, hi):
        return jnp.concatenate([_dot(g, w_ref[:, lo:hi]) for g in hn], axis=0)

    def rows_dot(a, w):
        return jnp.concatenate([_dot(a[r:r + MERGE_ROWS], w) for r in range(0, tm, MERGE_ROWS)], axis=0)

    @pl.when(i == 0)
    def _():
        cext_ref[0:CONV_HALO, :] = jnp.zeros((CONV_HALO, CONV_CH), F32)
        pext_ref[0:POOL_HALO, :] = jnp.zeros((POOL_HALO, POOL_CH), F32)
        for j in range(CONV_W):
            cwb_ref[j] = jnp.broadcast_to(cw_ref[j:j + 1, :], (SUBLANES, CONV_CH))

    cext_ref[CONV_HALO:CONV_HALO + tm, :] = proj(wwide_ref, *SEG_CA) * jax.nn.sigmoid(proj(wwide_ref, *SEG_CG))
    ext_rows = tm + CONV_HALO
    ext = cext_ref[...]
    for r in range(1, SUBLANES):
        cshift_ref[r - 1] = pltpu.roll(ext, ext_rows - r, 0)[0:ext_rows - SUBLANES]
    first_shift = CONV_HALO - (CONV_W - 1)
    n_conv_steps = tm // CONV_ROWS
    gate_slabs = (SEG_GATES[1] - SEG_GATES[0]) // MXU_DIM
    slab = 0
    for step in range(n_conv_steps):
        r0 = step * CONV_ROWS
        accs = [None] * (CONV_ROWS // SUBLANES)
        for j in range(CONV_W):
            s = first_shift + j
            r, base = s % SUBLANES, r0 + s - s % SUBLANES
            w_tile = cwb_ref[j]
            for g in range(len(accs)):
                rows = slice(base + g * SUBLANES, base + (g + 1) * SUBLANES)
                src = cext_ref[rows, :] if r == 0 else cshift_ref[r - 1, rows, :]
                accs[g] = src * w_tile if accs[g] is None else accs[g] + src * w_tile
        conv_ref[r0:r0 + CONV_ROWS, :] = jnp.concatenate(accs, axis=0)
        slab_end = (step + 1) * gate_slabs // n_conv_steps
        while slab < slab_end:
            lo = slab * MXU_DIM
            gates_ref[:, lo:lo + MXU_DIM] = jax.nn.sigmoid(
                proj(wwide_ref, SEG_GATES[0] + lo, SEG_GATES[0] + lo + MXU_DIM))
            slab += 1
    cext_ref[0:CONV_HALO, :] = cext_ref[tm:tm + CONV_HALO, :]

    cos = cos_ref[0]
    sin = sin_ref[0]

    cq = _rms(proj(wlat_ref, *SEG_Q), qn_ref[...]).astype(BF16)
    qt = _dot(wuqt_ref[...], cq, _NT)
    for h in range(N_HEADS):
        base = h * HEAD_PAD
        r1, r2 = _rope_rows(qt[base + NOPE:base + NOPE + ROPE_HALF], qt[base + NOPE + ROPE_HALF:base + QK_DIM],
                            cos, sin)
        blk = jnp.concatenate([qt[base:base + NOPE], r1, r2, qt[base + QK_DIM:base + HEAD_PAD]], axis=0)
        qt_ref[0, base:base + HEAD_PAD, :] = (blk * Q_SCALE).astype(BF16)

    ckv = _rms(proj(wlat_ref, *SEG_KV), kvn_ref[...]).astype(BF16)
    kpad = rows_dot(ckv, wuk_ref[...])
    row_id = lax.broadcasted_iota(jnp.int32, (N_HEADS * V_ROWS, 1), 0)
    ones_row = sum((row_id == h * V_ROWS + V_DIM).astype(F32) for h in range(N_HEADS))
    vt_ref[0] = (_dot(wuvt_ref[...], ckv, _NT) + ones_row).astype(BF16)
    krt = proj(wkr_ref, 0, HEAD_PAD).T
    r1, r2 = _rope_rows(krt[0:ROPE_HALF], krt[ROPE_HALF:ROPE], cos, sin)
    kr = jnp.concatenate([jnp.zeros((NOPE, tm), F32), r1, r2, jnp.zeros((HEAD_PAD - QK_DIM, tm), F32)], axis=0).T
    for h in range(N_HEADS):
        base = h * HEAD_PAD
        k_ref[0, :, base:base + HEAD_PAD] = (kpad[:, base:base + HEAD_PAD] + kr).astype(BF16)

    hc = conv_ref[...] + cb_ref[...]
    xc = hc - jnp.mean(hc, axis=-1, keepdims=True)
    ln = xc * lax.rsqrt(jnp.mean(xc * xc, axis=-1, keepdims=True) + EPS) * lng_ref[...] + lnb_ref[...]
    yconv = rows_dot((ln * jax.nn.sigmoid(ln)).astype(BF16), wco_ref[...])

    u = proj(wwide_ref, *SEG_POOL)
    pext_ref[POOL_HALO:POOL_HALO + tm, :] = u
    t1 = i * tm + 1 + lax.broadcasted_iota(jnp.int32, (tm, 1), 0)
    pool_rows = tm + POOL_HALO
    run = pext_ref[...]
    diffs = []
    for gi, w in enumerate(POOL_WINDOWS):
        run = run + pltpu.roll(run, w // 2, 0)
        tot = run[POOL_HALO:pool_rows, 0:POOL_GD]
        cnt = jnp.minimum(t1, w).astype(F32)
        diffs.append(tot / cnt - u[:, gi * POOL_GD:(gi + 1) * POOL_GD])
        run = run[:, POOL_GD:]
    pext_ref[0:POOL_HALO, :] = pext_ref[tm:tm + POOL_HALO, :]
    dmix = jnp.concatenate(diffs, axis=1).astype(BF16)
    mixed = jnp.concatenate([rows_dot(dmix[:, gi * POOL_GD:(gi + 1) * POOL_GD], pw_ref[gi])
                             for gi in range(len(POOL_WINDOWS))], axis=1) * ps_ref[...]
    ypool = rows_dot(mixed.astype(BF16), wpo_ref[...])

    part = gates_ref[:, D_MODEL:2 * D_MODEL] * yconv + gates_ref[:, 2 * D_MODEL:3 * D_MODEL] * ypool
    part_ref[0] = part.astype(BF16)
    g0_ref[0] = gates_ref[:, 0:D_MODEL].astype(BF16)


def _layer_spec(stacked, l):
    zeros = (0,) * (stacked.ndim - 1)
    return pl.BlockSpec((None,) + stacked.shape[1:], lambda b, i: (l,) + zeros, pipeline_mode=pl.Buffered(1))


def _mixer_in(x, cos_t, sin_t, p, l, *, tm):
    b, s, d = x.shape
    nt = s // tm
    row_tile = lambda w: pl.BlockSpec((1, tm, w), lambda bi, i: (bi, i, 0))
    col_tile = lambda r: pl.BlockSpec((1, r, tm), lambda bi, i: (bi, 0, i))
    assert s % tm == 0 and tm % MERGE_ROWS == 0 and tm % CONV_ROWS == 0
    consts = [p[name] for name in (
        "npre", "w_lat", "w_kr", "w_wide", "q_norm", "w_uq_t", "kv_norm", "w_uk", "w_uv_t", "conv_w", "conv_b",
        "ln_g", "ln_b", "w_conv_o", "pool_w", "pool_scale", "w_pool_o")]
    return pl.pallas_call(
        functools.partial(_mixer_in_kernel, tm=tm),
        grid=(b, nt),
        in_specs=[row_tile(d), col_tile(ROPE_HALF), col_tile(ROPE_HALF)] + [_layer_spec(c, l) for c in consts],
        out_specs=[col_tile(N_HEADS * HEAD_PAD), row_tile(N_HEADS * HEAD_PAD), col_tile(N_HEADS * V_ROWS),
                   row_tile(d), row_tile(d)],
        out_shape=[
            jax.ShapeDtypeStruct((b, N_HEADS * HEAD_PAD, s), BF16),
            jax.ShapeDtypeStruct((b, s, N_HEADS * HEAD_PAD), BF16),
            jax.ShapeDtypeStruct((b, N_HEADS * V_ROWS, s), BF16),
            jax.ShapeDtypeStruct((b, s, d), BF16),
            jax.ShapeDtypeStruct((b, s, d), BF16),
        ],
        scratch_shapes=[
            pltpu.VMEM((tm + CONV_HALO, CONV_CH), F32),
            pltpu.VMEM((SUBLANES - 1, tm + CONV_HALO - SUBLANES, CONV_CH), F32),
            pltpu.VMEM((CONV_W, SUBLANES, CONV_CH), F32),
            pltpu.VMEM((tm, CONV_CH), F32),
            pltpu.VMEM((tm + POOL_HALO, POOL_CH), F32),
            pltpu.VMEM((tm, N_BRANCHES * D_MODEL), F32),
        ],
        compiler_params=pltpu.CompilerParams(
            dimension_semantics=("arbitrary", "arbitrary"), vmem_limit_bytes=VMEM_LIMIT_BYTES),
        name="mixer_in",
    )(x, cos_t, sin_t, *consts)


def _attn_kernel(qt_ref, k_ref, vt_ref, ot_ref, s0_ref, s1_ref, m_ref, acc_ref, *, tq):
    i = pl.program_id(1)
    half = tq // 2
    m_ref[...] = jnp.full(m_ref.shape, -jnp.inf, F32)
    acc_ref[...] = jnp.zeros(acc_ref.shape, F32)

    def head(h):
        return slice(h * HEAD_PAD, (h + 1) * HEAD_PAD)

    def scores_into(s_ref, c):
        k0 = pl.multiple_of(c * tq, tq)
        for h in range(N_HEADS):
            s_ref[h] = _dot(k_ref[0, pl.ds(k0, tq), head(h)], qt_ref[0, head(h), :])

    def update(s_ref, c, k_lo, n_keys, q_lo, masked):
        k0 = pl.multiple_of(c * tq + k_lo, half)
        for h in range(N_HEADS):
            s = s_ref[h, k_lo:k_lo + n_keys, q_lo:tq]
            if masked:
                key = lax.broadcasted_iota(jnp.int32, (n_keys, 1), 0) + (k_lo - q_lo)
                qry = lax.broadcasted_iota(jnp.int32, (1, tq - q_lo), 1)
                s = jnp.where(key <= qry, s, -jnp.inf)
            m = m_ref[h, :, q_lo:tq]
            m_new = jnp.maximum(m, jnp.max(s, axis=0, keepdims=True))
            alpha = jnp.exp2(m - m_new)
            prob = jnp.exp2(s - m_new).astype(BF16)
            m_ref[h, :, q_lo:tq] = m_new
            vc = vt_ref[0, h * V_ROWS:(h + 1) * V_ROWS, pl.ds(k0, n_keys)]
            acc_ref[h, :, q_lo:tq] = alpha * acc_ref[h, :, q_lo:tq] + _dot(vc, prob)

    def diagonal(s_ref):
        update(s_ref, i, 0, half, 0, True)
        update(s_ref, i, half, half, half, True)

    def pair(p, carry):
        scores_into(s1_ref, 2 * p + 1)
        update(s0_ref, 2 * p, 0, tq, 0, False)
        scores_into(s0_ref, 2 * p + 2)
        update(s1_ref, 2 * p + 1, 0, tq, 0, False)
        return carry

    def tail_even():
        diagonal(s0_ref)

    def tail_odd():
        scores_into(s1_ref, i)
        update(s0_ref, i - 1, 0, tq, 0, False)
        diagonal(s1_ref)

    @pl.when(i < 2)
    def _():
        scores_into(s0_ref, 0)

    @pl.when(i >= 2)
    def _():
        scores_into(s0_ref, 0)
        pair(0, 0)

    lax.fori_loop(1, i // 2, pair, 0)

    @pl.when(i % 2 == 0)
    def _():
        tail_even()

    @pl.when(i % 2 == 1)
    def _():
        tail_odd()

    for h in range(N_HEADS):
        ot_ref[0, h * V_DIM:(h + 1) * V_DIM, :] = (acc_ref[h, 0:V_DIM, :] / acc_ref[h, V_DIM:V_DIM + 1, :]).astype(BF16)


def _attention(q_t, k, v_t, *, tq):
    b, _, s = q_t.shape
    tk = tq
    assert s % tq == 0
    return pl.pallas_call(
        functools.partial(_attn_kernel, tq=tq),
        grid=(b, s // tq),
        in_specs=[
            pl.BlockSpec((1, N_HEADS * HEAD_PAD, tq), lambda bi, i: (bi, 0, i)),
            pl.BlockSpec((1, s, N_HEADS * HEAD_PAD), lambda bi, i: (bi, 0, 0)),
            pl.BlockSpec((1, N_HEADS * V_ROWS, s), lambda bi, i: (bi, 0, 0)),
        ],
        out_specs=pl.BlockSpec((1, N_HEADS * V_DIM, tq), lambda bi, i: (bi, 0, i)),
        out_shape=jax.ShapeDtypeStruct((b, N_HEADS * V_DIM, s), BF16),
        scratch_shapes=[
            pltpu.VMEM((N_HEADS, tk, tq), F32),
            pltpu.VMEM((N_HEADS, tk, tq), F32),
            pltpu.VMEM((N_HEADS, 1, tq), F32),
            pltpu.VMEM((N_HEADS, V_ROWS, tq), F32),
        ],
        compiler_params=pltpu.CompilerParams(
            dimension_semantics=("arbitrary", "arbitrary"), vmem_limit_bytes=VMEM_LIMIT_BYTES),
        name="attention",
    )(q_t, k, v_t)


def _merge_ffn_kernel(h_ref, ot_ref, g0_ref, part_ref, wao_ref, wmo_ref, npost_ref, fpre_ref, wg_ref, wu_ref,
                      wd_ref, fpost_ref, out_ref):
    tm = h_ref.shape[1]
    groups = [slice(r, r + MERGE_ROWS) for r in range(0, tm, MERGE_ROWS)]
    h1 = []
    for rows in groups:
        yattn = _dot(ot_ref[0, :, rows], wao_ref[...], _TN)
        merged = g0_ref[0, rows, :].astype(F32) * yattn + part_ref[0, rows, :].astype(F32)
        h1.append(h_ref[0, rows, :] + _rms(_dot(merged.astype(BF16), wmo_ref[...]), npost_ref[...]))
    act = []
    for g in range(len(groups)):
        hn = _rms(h1[g], fpre_ref[...]).astype(BF16)
        gate = _dot(hn, wg_ref[...])
        act.append((gate * jax.nn.sigmoid(gate) * _dot(hn, wu_ref[...])).astype(BF16))
    for g, rows in enumerate(groups):
        out_ref[0, rows, :] = h1[g] + _rms(_dot(act[g], wd_ref[...]), fpost_ref[...])


def _merge_ffn(h, o_t, g0, part, p, l, *, tm):
    b, s, d = h.shape
    row_tile = lambda w: pl.BlockSpec((1, tm, w), lambda bi, i: (bi, i, 0))
    assert s % tm == 0 and tm % MERGE_ROWS == 0
    consts = [p["w_attn_o"], p["w_mix_o"], p["npost"], p["fpre"], p["w_gate"], p["w_up"], p["w_down"], p["fpost"]]
    return pl.pallas_call(
        _merge_ffn_kernel,
        grid=(b, s // tm),
        in_specs=[row_tile(d), pl.BlockSpec((1, N_HEADS * V_DIM, tm), lambda bi, i: (bi, 0, i)), row_tile(d),
                  row_tile(d)] + [_layer_spec(c, l) for c in consts],
        out_specs=row_tile(d),
        out_shape=jax.ShapeDtypeStruct((b, s, d), F32),
        compiler_params=pltpu.CompilerParams(
            dimension_semantics=("arbitrary", "arbitrary"), vmem_limit_bytes=VMEM_LIMIT_BYTES),
        name="merge_ffn",
    )(h, o_t, g0, part, *consts)


def _stacked_params(mix_norm_pre, w_in, q_norm, w_uq, kv_norm, w_uk, w_uv, w_attn_o, conv_w, conv_b, conv_ln_g,
                    conv_ln_b, w_conv_o, pool_w, pool_scale, w_pool_o, w_mix_o, mix_norm_post, ffn_norm_pre, w_gate,
                    w_up, w_down, ffn_norm_post):
    depth = w_in.shape[0]
    row = lambda v: v.reshape(depth, 1, -1)
    w_in = w_in.astype(BF16)
    w_kr = jnp.pad(w_in[:, :, LAT_COLS:WIDE_START], ((0, 0), (0, 0), (0, HEAD_PAD - ROPE)))
    assert w_in.shape[2] - WIDE_START == SEG_GATES[1]

    def pad_heads(w, dh, dh_pad):
        w = jnp.pad(w.reshape(depth, w.shape[1], N_HEADS, dh), ((0, 0), (0, 0), (0, 0), (0, dh_pad - dh)))
        return w.reshape(depth, w.shape[1], N_HEADS * dh_pad)

    transpose = lambda w: jnp.swapaxes(w, 1, 2)
    return dict(
        npre=row(mix_norm_pre), w_lat=w_in[:, :, :LAT_COLS], w_kr=w_kr,
        w_wide=w_in[:, :, WIDE_START:], q_norm=row(q_norm),
        w_uq_t=transpose(pad_heads(w_uq, QK_DIM, HEAD_PAD)).astype(BF16), kv_norm=row(kv_norm),
        w_uk=pad_heads(w_uk, NOPE, HEAD_PAD).astype(BF16),
        w_uv_t=transpose(pad_heads(w_uv, V_DIM, V_ROWS)).astype(BF16),
        conv_w=conv_w, conv_b=row(conv_b), ln_g=row(conv_ln_g), ln_b=row(conv_ln_b),
        w_conv_o=w_conv_o.astype(BF16), pool_w=pool_w.astype(BF16), pool_scale=row(pool_scale),
        w_pool_o=w_pool_o.astype(BF16),
        w_attn_o=w_attn_o.astype(BF16), w_mix_o=w_mix_o.astype(BF16), npost=row(mix_norm_post),
        fpre=row(ffn_norm_pre), w_gate=w_gate.astype(BF16), w_up=w_up.astype(BF16),
        w_down=w_down.astype(BF16), fpost=row(ffn_norm_post),
    )


def kernel(x, positions, mix_norm_pre, w_in, q_norm, w_uq, kv_norm, w_uk, w_uv, w_attn_o, conv_w, conv_b, conv_ln_g, conv_ln_b, w_conv_o, pool_w, pool_scale, w_pool_o, w_mix_o, mix_norm_post, ffn_norm_pre, w_gate, w_up, w_down, ffn_norm_post):
    p = _stacked_params(mix_norm_pre, w_in, q_norm, w_uq, kv_norm, w_uk, w_uv, w_attn_o, conv_w, conv_b, conv_ln_g,
                        conv_ln_b, w_conv_o, pool_w, pool_scale, w_pool_o, w_mix_o, mix_norm_post, ffn_norm_pre,
                        w_gate, w_up, w_down, ffn_norm_post)
    assert x.dtype == F32 and x.shape[2] == D_MODEL and positions.shape == x.shape[:2]
    s = x.shape[1]
    tm = min(ROW_TILE, s)
    cos_t, sin_t = _rope_tables(positions)
    h = x
    for l in range(w_in.shape[0]):
        q_t, k, v_t, g0, part = _mixer_in(h, cos_t, sin_t, p, l, tm=tm)
        o_t = _attention(q_t, k, v_t, tq=min(ATTN_TQ, s))
        h = _merge_ffn(h, o_t, g0, part, p, l, tm=tm)
    return h
```

```python
import functools
import math

import jax
import jax.numpy as jnp
from jax import lax
from jax.experimental import pallas as pl
from jax.experimental.pallas import tpu as pltpu

D_MODEL = 1024
N_HEADS = 8
NOPE = 64
ROPE = 32
ROPE_HALF = ROPE // 2
V_DIM = 64
Q_RANK = 384
KV_RANK = 256
ROPE_THETA = 10000.0
CONV_CH = 512
CONV_W = 31
POOL_WINDOWS = (2, 4, 8, 16)
POOL_CH = 512
POOL_GD = POOL_CH // len(POOL_WINDOWS)
N_BRANCHES = 3
D_FF = 2816
EPS = 1e-6

LANES = 128
SUBLANES = 8
BF16_ROWS = 16
MXU_DIM = 256
VMEM_LIMIT_BYTES = 56 * 1024 * 1024

HEAD_PAD = LANES
QK_DIM = NOPE + ROPE
V_ROWS = V_DIM + BF16_ROWS
Q_SCALE = (1.0 / math.sqrt(QK_DIM)) * math.log2(math.e)

LAT_COLS = Q_RANK + KV_RANK
WIDE_START = LAT_COLS + ROPE
SEG_Q = (0, Q_RANK)
SEG_KV = (Q_RANK, LAT_COLS)
SEG_CA = (0, CONV_CH)
SEG_CG = (CONV_CH, 2 * CONV_CH)
SEG_POOL = (2 * CONV_CH, 2 * CONV_CH + POOL_CH)
SEG_GATES = (SEG_POOL[1], SEG_POOL[1] + N_BRANCHES * D_MODEL)

CONV_HALO = 32
POOL_HALO = 16
CONV_ROWS = 32

ROW_TILE = 512
MERGE_ROWS = 256
ATTN_TQ = 512
ATTN_HEADS = 4

F32 = jnp.float32
BF16 = jnp.bfloat16

_NN = (((1,), (0,)), ((), ()))
_NT = (((1,), (1,)), ((), ()))
_TN = (((0,), (0,)), ((), ()))


def _rms(x, g):
    return x * lax.rsqrt(jnp.mean(x * x, axis=-1, keepdims=True) + EPS) * g


def _dot(a, b, dims=_NN):
    return lax.dot_general(a, b, dims, preferred_element_type=F32)


def _rope_rows(x1, x2, cos, sin):
    return x1 * cos - x2 * sin, x2 * cos + x1 * sin


def _rope_table_kernel(pos_ref, invf_ref, cos_ref, sin_ref):
    ang = invf_ref[...] * pos_ref[0].astype(F32)
    cos_ref[0] = jnp.cos(ang)
    sin_ref[0] = jnp.sin(ang)


def _rope_tables(positions):
    b, s = positions.shape
    inv_freq = ROPE_THETA ** (-jnp.arange(0, ROPE, 2, dtype=F32) / ROPE)
    out = jax.ShapeDtypeStruct((b, ROPE_HALF, s), F32)
    return pl.pallas_call(
        _rope_table_kernel,
        grid=(b,),
        in_specs=[
            pl.BlockSpec((1, 1, s), lambda i: (i, 0, 0)),
            pl.BlockSpec((ROPE_HALF, 1), lambda i: (0, 0)),
        ],
        out_specs=[
            pl.BlockSpec((1, ROPE_HALF, s), lambda i: (i, 0, 0)),
            pl.BlockSpec((1, ROPE_HALF, s), lambda i: (i, 0, 0)),
        ],
        out_shape=[out, out],
        name="rope_tables",
    )(positions.reshape(b, 1, s), inv_freq.reshape(ROPE_HALF, 1))


def _mixer_in_kernel(x_ref, cos_ref, sin_ref, npre_ref, wlat_ref, wkr_ref, wwide_ref, qn_ref, wuqt_ref, kvn_ref,
                     wuk_ref, wuvt_ref, cw_ref, cb_ref, lng_ref, lnb_ref, wco_ref, pw_ref, ps_ref, wpo_ref,
                     qt_ref, k_ref, vt_ref, g0_ref, part_ref,
                     cext_ref, cshift_ref, cwb_ref, conv_ref, pext_ref, gates_ref, *, tm):
    i = pl.program_id(1)
    hn = [_rms(x_ref[0, r:r + MERGE_ROWS, :], npre_ref[...]).astype(BF16) for r in range(0, tm, MERGE_ROWS)]

    def proj(w_ref, lo, hi):
        return jnp.concatenate([_dot(g, w_ref[:, lo:hi]) for g in hn], axis=0)

    def rows_dot(a, w):
        return jnp.concatenate([_dot(a[r:r + MERGE_ROWS], w) for r in range(0, tm, MERGE_ROWS)], axis=0)

    @pl.when(i == 0)
    def _():
        cext_ref[0:CONV_HALO, :] = jnp.zeros((CONV_HALO, CONV_CH), F32)
        pext_ref[0:POOL_HALO, :] = jnp.zeros((POOL_HALO, POOL_CH), F32)
        for j in range(CONV_W):
            cwb_ref[j] = jnp.broadcast_to(cw_ref[j:j + 1, :], (SUBLANES, CONV_CH))

    cext_ref[CONV_HALO:CONV_HALO + tm, :] = proj(wwide_ref, *SEG_CA) * jax.nn.sigmoid(proj(wwide_ref, *SEG_CG))
    ext_rows = tm + CONV_HALO
    ext = cext_ref[...]
    for r in range(1, SUBLANES):
        cshift_ref[r - 1] = pltpu.roll(ext, ext_rows - r, 0)[0:ext_rows - SUBLANES]
    first_shift = CONV_HALO - (CONV_W - 1)
    n_conv_steps = tm // CONV_ROWS
    gate_slabs = (SEG_GATES[1] - SEG_GATES[0]) // MXU_DIM
    slab = 0
    for step in range(n_conv_steps):
        r0 = step * CONV_ROWS
        accs = [None] * (CONV_ROWS // SUBLANES)
        for j in range(CONV_W):
            s = first_shift + j
            r, base = s % SUBLANES, r0 + s - s % SUBLANES
            w_tile = cwb_ref[j]
            for g in range(len(accs)):
                rows = slice(base + g * SUBLANES, base + (g + 1) * SUBLANES)
                src = cext_ref[rows, :] if r == 0 else cshift_ref[r - 1, rows, :]
                accs[g] = src * w_tile if accs[g] is None else accs[g] + src * w_tile
        conv_ref[r0:r0 + CONV_ROWS, :] = jnp.concatenate(accs, axis=0)
        slab_end = (step + 1) * gate_slabs // n_conv_steps
        while slab < slab_end:
            lo = slab * MXU_DIM
            gates_ref[:, lo:lo + MXU_DIM] = jax.nn.sigmoid(
                proj(wwide_ref, SEG_GATES[0] + lo, SEG_GATES[0] + lo + MXU_DIM))
            slab += 1
    cext_ref[0:CONV_HALO, :] = cext_ref[tm:tm + CONV_HALO, :]

    cos = cos_ref[0]
    sin = sin_ref[0]

    cq = _rms(proj(wlat_ref, *SEG_Q), qn_ref[...]).astype(BF16)
    qt = _dot(wuqt_ref[...], cq, _NT)
    for h in range(N_HEADS):
        base = h * HEAD_PAD
        r1, r2 = _rope_rows(qt[base + NOPE:base + NOPE + ROPE_HALF], qt[base + NOPE + ROPE_HALF:base + QK_DIM],
                            cos, sin)
        blk = jnp.concatenate([qt[base:base + NOPE], r1, r2, qt[base + QK_DIM:base + HEAD_PAD]], axis=0)
        qt_ref[0, base:base + HEAD_PAD, :] = (blk * Q_SCALE).astype(BF16)

    ckv = _rms(proj(wlat_ref, *SEG_KV), kvn_ref[...]).astype(BF16)
    kpad = rows_dot(ckv, wuk_ref[...])
    row_id = lax.broadcasted_iota(jnp.int32, (N_HEADS * V_ROWS, 1), 0)
    ones_row = sum((row_id == h * V_ROWS + V_DIM).astype(F32) for h in range(N_HEADS))
    vt_ref[0] = (_dot(wuvt_ref[...], ckv, _NT) + ones_row).astype(BF16)
    krt = proj(wkr_ref, 0, HEAD_PAD).T
    r1, r2 = _rope_rows(krt[0:ROPE_HALF], krt[ROPE_HALF:ROPE], cos, sin)
    kr = jnp.concatenate([jnp.zeros((NOPE, tm), F32), r1, r2, jnp.zeros((HEAD_PAD - QK_DIM, tm), F32)], axis=0).T
    for h in range(N_HEADS):
        base = h * HEAD_PAD
        k_ref[0, :, base:base + HEAD_PAD] = (kpad[:, base:base + HEAD_PAD] + kr).astype(BF16)

    hc = conv_ref[...] + cb_ref[...]
    xc = hc - jnp.mean(hc, axis=-1, keepdims=True)
    ln = xc * lax.rsqrt(jnp.mean(xc * xc, axis=-1, keepdims=True) + EPS) * lng_ref[...] + lnb_ref[...]
    yconv = rows_dot((ln * jax.nn.sigmoid(ln)).astype(BF16), wco_ref[...])

    u = proj(wwide_ref, *SEG_POOL)
    pext_ref[POOL_HALO:POOL_HALO + tm, :] = u
    t1 = i * tm + 1 + lax.broadcasted_iota(jnp.int32, (tm, 1), 0)
    pool_rows = tm + POOL_HALO
    run = pext_ref[...]
    diffs = []
    for gi, w in enumerate(POOL_WINDOWS):
        run = run + pltpu.roll(run, w // 2, 0)
        tot = run[POOL_HALO:pool_rows, 0:POOL_GD]
        cnt = jnp.minimum(t1, w).astype(F32)
        diffs.append(tot / cnt - u[:, gi * POOL_GD:(gi + 1) * POOL_GD])
        run = run[:, POOL_GD:]
    pext_ref[0:POOL_HALO, :] = pext_ref[tm:tm + POOL_HALO, :]
    dmix = jnp.concatenate(diffs, axis=1).astype(BF16)
    mixed = jnp.concatenate([rows_dot(dmix[:, gi * POOL_GD:(gi + 1) * POOL_GD], pw_ref[gi])
                             for gi in range(len(POOL_WINDOWS))], axis=1) * ps_ref[...]
    ypool = rows_dot(mixed.astype(BF16), wpo_ref[...])

    part = gates_ref[:, D_MODEL:2 * D_MODEL] * yconv + gates_ref[:, 2 * D_MODEL:3 * D_MODEL] * ypool
    part_ref[0] = part.astype(BF16)
    g0_ref[0] = gates_ref[:, 0:D_MODEL].astype(BF16)


def _layer_spec(stacked, l):
    zeros = (0,) * (stacked.ndim - 1)
    return pl.BlockSpec((None,) + stacked.shape[1:], lambda b, i: (l,) + zeros, pipeline_mode=pl.Buffered(1))


def _mixer_in(x, cos_t, sin_t, p, l, *, tm):
    b, s, d = x.shape
    nt = s // tm
    row_tile = lambda w: pl.BlockSpec((1, tm, w), lambda bi, i: (bi, i, 0))
    col_tile = lambda r: pl.BlockSpec((1, r, tm), lambda bi, i: (bi, 0, i))
    assert s % tm == 0 and tm % MERGE_ROWS == 0 and tm % CONV_ROWS == 0
    consts = [p[name] for name in (
        "npre", "w_lat", "w_kr", "w_wide", "q_norm", "w_uq_t", "kv_norm", "w_uk", "w_uv_t", "conv_w", "conv_b",
        "ln_g", "ln_b", "w_conv_o", "pool_w", "pool_scale", "w_pool_o")]
    return pl.pallas_call(
        functools.partial(_mixer_in_kernel, tm=tm),
        grid=(b, nt),
        in_specs=[row_tile(d), col_tile(ROPE_HALF), col_tile(ROPE_HALF)] + [_layer_spec(c, l) for c in consts],
        out_specs=[col_tile(N_HEADS * HEAD_PAD), row_tile(N_HEADS * HEAD_PAD), col_tile(N_HEADS * V_ROWS),
                   row_tile(d), row_tile(d)],
        out_shape=[
            jax.ShapeDtypeStruct((b, N_HEADS * HEAD_PAD, s), BF16),
            jax.ShapeDtypeStruct((b, s, N_HEADS * HEAD_PAD), BF16),
            jax.ShapeDtypeStruct((b, N_HEADS * V_ROWS, s), BF16),
            jax.ShapeDtypeStruct((b, s, d), BF16),
            jax.ShapeDtypeStruct((b, s, d), BF16),
        ],
        scratch_shapes=[
            pltpu.VMEM((tm + CONV_HALO, CONV_CH), F32),
            pltpu.VMEM((SUBLANES - 1, tm + CONV_HALO - SUBLANES, CONV_CH), F32),
            pltpu.VMEM((CONV_W, SUBLANES, CONV_CH), F32),
            pltpu.VMEM((tm, CONV_CH), F32),
            pltpu.VMEM((tm + POOL_HALO, POOL_CH), F32),
            pltpu.VMEM((tm, N_BRANCHES * D_MODEL), F32),
        ],
        compiler_params=pltpu.CompilerParams(
            dimension_semantics=("arbitrary", "arbitrary"), vmem_limit_bytes=VMEM_LIMIT_BYTES),
        name="mixer_in",
    )(x, cos_t, sin_t, *consts)


def _attn_kernel(qt_ref, k_ref, vt_ref, ot_ref, s0_ref, s1_ref, m_ref, acc_ref, *, tq):
    i = pl.program_id(2)
    half = tq // 2
    m_ref[...] = jnp.full(m_ref.shape, -jnp.inf, F32)
    acc_ref[...] = jnp.zeros(acc_ref.shape, F32)

    def head(h):
        return slice(h * HEAD_PAD, (h + 1) * HEAD_PAD)

    def scores_into(s_ref, c):
        k0 = pl.multiple_of(c * tq, tq)
        for h in range(ATTN_HEADS):
            s_ref[h] = _dot(k_ref[0, pl.ds(k0, tq), head(h)], qt_ref[0, head(h), :])

    def update(s_ref, c, k_lo, n_keys, q_lo, masked):
        k0 = pl.multiple_of(c * tq + k_lo, half)
        for h in range(ATTN_HEADS):
            s = s_ref[h, k_lo:k_lo + n_keys, q_lo:tq]
            if masked:
                key = lax.broadcasted_iota(jnp.int32, (n_keys, 1), 0) + (k_lo - q_lo)
                qry = lax.broadcasted_iota(jnp.int32, (1, tq - q_lo), 1)
                s = jnp.where(key <= qry, s, -jnp.inf)
            m = m_ref[h, :, q_lo:tq]
            m_new = jnp.maximum(m, jnp.max(s, axis=0, keepdims=True))
            alpha = jnp.exp2(m - m_new)
            prob = jnp.exp2(s - m_new).astype(BF16)
            m_ref[h, :, q_lo:tq] = m_new
            vc = vt_ref[0, h * V_ROWS:(h + 1) * V_ROWS, pl.ds(k0, n_keys)]
            acc_ref[h, :, q_lo:tq] = alpha * acc_ref[h, :, q_lo:tq] + _dot(vc, prob)

    def diagonal(s_ref):
        update(s_ref, i, 0, half, 0, True)
        update(s_ref, i, half, half, half, True)

    def pair(p, carry):
        scores_into(s1_ref, 2 * p + 1)
        update(s0_ref, 2 * p, 0, tq, 0, False)
        scores_into(s0_ref, 2 * p + 2)
        update(s1_ref, 2 * p + 1, 0, tq, 0, False)
        return carry

    def tail_even():
        diagonal(s0_ref)

    def tail_odd():
        scores_into(s1_ref, i)
        update(s0_ref, i - 1, 0, tq, 0, False)
        diagonal(s1_ref)

    @pl.when(i == 0)
    def _():
        scores_into(s0_ref, 0)
        tail_even()

    @pl.when(i == 1)
    def _():
        scores_into(s0_ref, 0)
        tail_odd()

    @pl.when(i >= 2)
    def _():
        scores_into(s0_ref, 0)
        pair(0, 0)

    lax.fori_loop(1, i // 2, pair, 0)

    @pl.when(jnp.logical_and(i >= 2, i % 2 == 0))
    def _():
        tail_even()

    @pl.when(jnp.logical_and(i >= 2, i % 2 == 1))
    def _():
        tail_odd()

    for h in range(ATTN_HEADS):
        ot_ref[0, h * V_DIM:(h + 1) * V_DIM, :] = (acc_ref[h, 0:V_DIM, :] / acc_ref[h, V_DIM:V_DIM + 1, :]).astype(BF16)


def _attention(q_t, k, v_t, *, tq):
    b, _, s = q_t.shape
    tk = tq
    assert s % tq == 0
    return pl.pallas_call(
        functools.partial(_attn_kernel, tq=tq),
        grid=(b, N_HEADS // ATTN_HEADS, s // tq),
        in_specs=[
            pl.BlockSpec((1, ATTN_HEADS * HEAD_PAD, tq), lambda bi, g, i: (bi, g, i)),
            pl.BlockSpec((1, s, ATTN_HEADS * HEAD_PAD), lambda bi, g, i: (bi, 0, g)),
            pl.BlockSpec((1, ATTN_HEADS * V_ROWS, s), lambda bi, g, i: (bi, g, 0)),
        ],
        out_specs=pl.BlockSpec((1, ATTN_HEADS * V_DIM, tq), lambda bi, g, i: (bi, g, i)),
        out_shape=jax.ShapeDtypeStruct((b, N_HEADS * V_DIM, s), BF16),
        scratch_shapes=[
            pltpu.VMEM((ATTN_HEADS, tk, tq), F32),
            pltpu.VMEM((ATTN_HEADS, tk, tq), F32),
            pltpu.VMEM((ATTN_HEADS, 1, tq), F32),
            pltpu.VMEM((ATTN_HEADS, V_ROWS, tq), F32),
        ],
        compiler_params=pltpu.CompilerParams(
            dimension_semantics=("arbitrary", "arbitrary", "arbitrary"), vmem_limit_bytes=VMEM_LIMIT_BYTES),
        name="attention",
    )(q_t, k, v_t)


def _merge_ffn_kernel(h_ref, ot_ref, g0_ref, part_ref, wao_ref, wmo_ref, npost_ref, fpre_ref, wg_ref, wu_ref,
                      wd_ref, fpost_ref, out_ref):
    tm = h_ref.shape[1]
    groups = [slice(r, r + MERGE_ROWS) for r in range(0, tm, MERGE_ROWS)]
    h1 = []
    for rows in groups:
        yattn = _dot(ot_ref[0, :, rows], wao_ref[...], _TN)
        merged = g0_ref[0, rows, :].astype(F32) * yattn + part_ref[0, rows, :].astype(F32)
        h1.append(h_ref[0, rows, :] + _rms(_dot(merged.astype(BF16), wmo_ref[...]), npost_ref[...]))
    act = []
    for g in range(len(groups)):
        hn = _rms(h1[g], fpre_ref[...]).astype(BF16)
        gate = _dot(hn, wg_ref[...])
        act.append((gate * jax.nn.sigmoid(gate) * _dot(hn, wu_ref[...])).astype(BF16))
    for g, rows in enumerate(groups):
        out_ref[0, rows, :] = h1[g] + _rms(_dot(act[g], wd_ref[...]), fpost_ref[...])


def _merge_ffn(h, o_t, g0, part, p, l, *, tm):
    b, s, d = h.shape
    row_tile = lambda w: pl.BlockSpec((1, tm, w), lambda bi, i: (bi, i, 0))
    assert s % tm == 0 and tm % MERGE_ROWS == 0
    consts = [p["w_attn_o"], p["w_mix_o"], p["npost"], p["fpre"], p["w_gate"], p["w_up"], p["w_down"], p["fpost"]]
    return pl.pallas_call(
        _merge_ffn_kernel,
        grid=(b, s // tm),
        in_specs=[row_tile(d), pl.BlockSpec((1, N_HEADS * V_DIM, tm), lambda bi, i: (bi, 0, i)), row_tile(d),
                  row_tile(d)] + [_layer_spec(c, l) for c in consts],
        out_specs=row_tile(d),
        out_shape=jax.ShapeDtypeStruct((b, s, d), F32),
        compiler_params=pltpu.CompilerParams(
            dimension_semantics=("arbitrary", "arbitrary"), vmem_limit_bytes=VMEM_LIMIT_BYTES),
        name="merge_ffn",
    )(h, o_t, g0, part, *consts)


def _stacked_params(mix_norm_pre, w_in, q_norm, w_uq, kv_norm, w_uk, w_uv, w_attn_o, conv_w, conv_b, conv_ln_g,
                    conv_ln_b, w_conv_o, pool_w, pool_scale, w_pool_o, w_mix_o, mix_norm_post, ffn_norm_pre, w_gate,
                    w_up, w_down, ffn_norm_post):
    depth = w_in.shape[0]
    row = lambda v: v.reshape(depth, 1, -1)
    w_in = w_in.astype(BF16)
    w_kr = jnp.pad(w_in[:, :, LAT_COLS:WIDE_START], ((0, 0), (0, 0), (0, HEAD_PAD - ROPE)))
    assert w_in.shape[2] - WIDE_START == SEG_GATES[1]

    def pad_heads(w, dh, dh_pad):
        w = jnp.pad(w.reshape(depth, w.shape[1], N_HEADS, dh), ((0, 0), (0, 0), (0, 0), (0, dh_pad - dh)))
        return w.reshape(depth, w.shape[1], N_HEADS * dh_pad)

    transpose = lambda w: jnp.swapaxes(w, 1, 2)
    return dict(
        npre=row(mix_norm_pre), w_lat=w_in[:, :, :LAT_COLS], w_kr=w_kr,
        w_wide=w_in[:, :, WIDE_START:], q_norm=row(q_norm),
        w_uq_t=transpose(pad_heads(w_uq, QK_DIM, HEAD_PAD)).astype(BF16), kv_norm=row(kv_norm),
        w_uk=pad_heads(w_uk, NOPE, HEAD_PAD).astype(BF16),
        w_uv_t=transpose(pad_heads(w_uv, V_DIM, V_ROWS)).astype(BF16),
        conv_w=conv_w, conv_b=row(conv_b), ln_g=row(conv_ln_g), ln_b=row(conv_ln_b),
        w_conv_o=w_conv_o.astype(BF16), pool_w=pool_w.astype(BF16), pool_scale=row(pool_scale),
        w_pool_o=w_pool_o.astype(BF16),
        w_attn_o=w_attn_o.astype(BF16), w_mix_o=w_mix_o.astype(BF16), npost=row(mix_norm_post),
        fpre=row(ffn_norm_pre), w_gate=w_gate.astype(BF16), w_up=w_up.astype(BF16),
        w_down=w_down.astype(BF16), fpost=row(ffn_norm_post),
    )


def kernel(x, positions, mix_norm_pre, w_in, q_norm, w_uq, kv_norm, w_uk, w_uv, w_attn_o, conv_w, conv_b, conv_ln_g, conv_ln_b, w_conv_o, pool_w, pool_scale, w_pool_o, w_mix_o, mix_norm_post, ffn_norm_pre, w_gate, w_up, w_down, ffn_norm_post):
    p = _stacked_params(mix_norm_pre, w_in, q_norm, w_uq, kv_norm, w_uk, w_uv, w_attn_o, conv_w, conv_b, conv_ln_g,
                        conv_ln_b, w_conv_o, pool_w, pool_scale, w_pool_o, w_mix_o, mix_norm_post, ffn_norm_pre,
                        w_gate, w_up, w_down, ffn_norm_post)
    assert x.dtype == F32 and x.shape[2] == D_MODEL and positions.shape == x.shape[:2]
    s = x.shape[1]
    tm = min(ROW_TILE, s)
    cos_t, sin_t = _rope_tables(positions)
    h = x
    for l in range(w_in.shape[0]):
        q_t, k, v_t, g0, part = _mixer_in(h, cos_t, sin_t, p, l, tm=tm)
        o_t = _attention(q_t, k, v_t, tq=min(ATTN_TQ, s))
        h = _merge_ffn(h, o_t, g0, part, p, l, tm=tm)
    return h
```

```python
import functools
import math

import jax
import jax.numpy as jnp
from jax import lax
from jax.experimental import pallas as pl
from jax.experimental.pallas import tpu as pltpu

D_MODEL = 1024
N_HEADS = 8
NOPE = 64
ROPE = 32
ROPE_HALF = ROPE // 2
V_DIM = 64
Q_RANK = 384
KV_RANK = 256
ROPE_THETA = 10000.0
CONV_CH = 512
CONV_W = 31
POOL_WINDOWS = (2, 4, 8, 16)
POOL_CH = 512
POOL_GD = POOL_CH // len(POOL_WINDOWS)
N_BRANCHES = 3
D_FF = 2816
EPS = 1e-6

LANES = 128
SUBLANES = 8
BF16_ROWS = 16
MXU_DIM = 256
VMEM_LIMIT_BYTES = 56 * 1024 * 1024

HEAD_PAD = LANES
QK_DIM = NOPE + ROPE
V_ROWS = V_DIM + BF16_ROWS
Q_SCALE = (1.0 / math.sqrt(QK_DIM)) * math.log2(math.e)

LAT_COLS = Q_RANK + KV_RANK
WIDE_START = LAT_COLS + ROPE
SEG_Q = (0, Q_RANK)
SEG_KV = (Q_RANK, LAT_COLS)
SEG_CA = (0, CONV_CH)
SEG_CG = (CONV_CH, 2 * CONV_CH)
SEG_POOL = (2 * CONV_CH, 2 * CONV_CH + POOL_CH)
SEG_GATES = (SEG_POOL[1], SEG_POOL[1] + N_BRANCHES * D_MODEL)

CONV_HALO = 32
POOL_HALO = 16
CONV_ROWS = 64

ROW_TILE = 512
MERGE_ROWS = 256
ATTN_TQ = 512

F32 = jnp.float32
BF16 = jnp.bfloat16

_NN = (((1,), (0,)), ((), ()))
_NT = (((1,), (1,)), ((), ()))
_TN = (((0,), (0,)), ((), ()))


def _rms(x, g):
    return x * lax.rsqrt(jnp.mean(x * x, axis=-1, keepdims=True) + EPS) * g


def _dot(a, b, dims=_NN):
    return lax.dot_general(a, b, dims, preferred_element_type=F32)


def _rope_rows(x1, x2, cos, sin):
    return x1 * cos - x2 * sin, x2 * cos + x1 * sin


def _rope_table_kernel(pos_ref, invf_ref, cos_ref, sin_ref):
    ang = invf_ref[...] * pos_ref[0].astype(F32)
    cos_ref[0] = jnp.cos(ang)
    sin_ref[0] = jnp.sin(ang)


def _rope_tables(positions):
    b, s = positions.shape
    inv_freq = ROPE_THETA ** (-jnp.arange(0, ROPE, 2, dtype=F32) / ROPE)
    out = jax.ShapeDtypeStruct((b, ROPE_HALF, s), F32)
    return pl.pallas_call(
        _rope_table_kernel,
        grid=(b,),
        in_specs=[
            pl.BlockSpec((1, 1, s), lambda i: (i, 0, 0)),
            pl.BlockSpec((ROPE_HALF, 1), lambda i: (0, 0)),
        ],
        out_specs=[
            pl.BlockSpec((1, ROPE_HALF, s), lambda i: (i, 0, 0)),
            pl.BlockSpec((1, ROPE_HALF, s), lambda i: (i, 0, 0)),
        ],
        out_shape=[out, out],
        name="rope_tables",
    )(positions.reshape(b, 1, s), inv_freq.reshape(ROPE_HALF, 1))


def _mixer_in_kernel(x_ref, cos_ref, sin_ref, npre_ref, wlat_ref, wkr_ref, wwide_ref, qn_ref, wuqt_ref, kvn_ref,
                     wuk_ref, wuvt_ref, cw_ref, cb_ref, lng_ref, lnb_ref, wco_ref, pw_ref, ps_ref, wpo_ref,
                     qt_ref, k_ref, vt_ref, g0_ref, part_ref,
                     cext_ref, cshift_ref, cwb_ref, conv_ref, pext_ref, gates_ref, *, tm):
    i = pl.program_id(1)
    hn = [_rms(x_ref[0, r:r + MERGE_ROWS, :], npre_ref[...]).astype(BF16) for r in range(0, tm, MERGE_ROWS)]

    def proj(w_ref, lo, hi):
        return jnp.concatenate([_dot(g, w_ref[:, lo:hi]) for g in hn], axis=0)

    def rows_dot(a, w):
        return jnp.concatenate([_dot(a[r:r + MERGE_ROWS], w) for r in range(0, tm, MERGE_ROWS)], axis=0)

    @pl.when(i == 0)
    def _():
        cext_ref[0:CONV_HALO, :] = jnp.zeros((CONV_HALO, CONV_CH), F32)
        pext_ref[0:POOL_HALO, :] = jnp.zeros((POOL_HALO, POOL_CH), F32)
        for j in range(CONV_W):
            cwb_ref[j] = jnp.broadcast_to(cw_ref[j:j + 1, :], (SUBLANES, CONV_CH))

    cext_ref[CONV_HALO:CONV_HALO + tm, :] = proj(wwide_ref, *SEG_CA) * jax.nn.sigmoid(proj(wwide_ref, *SEG_CG))
    ext_rows = tm + CONV_HALO
    ext = cext_ref[...]
    for r in range(1, SUBLANES):
        cshift_ref[r - 1] = pltpu.roll(ext, ext_rows - r, 0)[0:ext_rows - SUBLANES]
    first_shift = CONV_HALO - (CONV_W - 1)
    n_conv_steps = tm // CONV_ROWS
    gate_slabs = (SEG_GATES[1] - SEG_GATES[0]) // MXU_DIM
    slab = 0
    for step in range(n_conv_steps):
        r0 = step * CONV_ROWS
        accs = [None] * (CONV_ROWS // SUBLANES)
        for j in range(CONV_W):
            s = first_shift + j
            r, base = s % SUBLANES, r0 + s - s % SUBLANES
            w_tile = cwb_ref[j]
            for g in range(len(accs)):
                rows = slice(base + g * SUBLANES, base + (g + 1) * SUBLANES)
                src = cext_ref[rows, :] if r == 0 else cshift_ref[r - 1, rows, :]
                accs[g] = src * w_tile if accs[g] is None else accs[g] + src * w_tile
        conv_ref[r0:r0 + CONV_ROWS, :] = jnp.concatenate(accs, axis=0)
        slab_end = (step + 1) * gate_slabs // n_conv_steps
        while slab < slab_end:
            lo = slab * MXU_DIM
            gates_ref[:, lo:lo + MXU_DIM] = jax.nn.sigmoid(
                proj(wwide_ref, SEG_GATES[0] + lo, SEG_GATES[0] + lo + MXU_DIM))
            slab += 1
    cext_ref[0:CONV_HALO, :] = cext_ref[tm:tm + CONV_HALO, :]

    cos = cos_ref[0]
    sin = sin_ref[0]

    cq = _rms(proj(wlat_ref, *SEG_Q), qn_ref[...]).astype(BF16)
    qt = _dot(wuqt_ref[...], cq, _NT)
    for h in range(N_HEADS):
        base = h * HEAD_PAD
        r1, r2 = _rope_rows(qt[base + NOPE:base + NOPE + ROPE_HALF], qt[base + NOPE + ROPE_HALF:base + QK_DIM],
                            cos, sin)
        blk = jnp.concatenate([qt[base:base + NOPE], r1, r2, qt[base + QK_DIM:base + HEAD_PAD]], axis=0)
        qt_ref[0, base:base + HEAD_PAD, :] = (blk * Q_SCALE).astype(BF16)

    ckv = _rms(proj(wlat_ref, *SEG_KV), kvn_ref[...]).astype(BF16)
    kpad = rows_dot(ckv, wuk_ref[...])
    row_id = lax.broadcasted_iota(jnp.int32, (N_HEADS * V_ROWS, 1), 0)
    ones_row = sum((row_id == h * V_ROWS + V_DIM).astype(F32) for h in range(N_HEADS))
    vt_ref[0] = (_dot(wuvt_ref[...], ckv, _NT) + ones_row).astype(BF16)
    krt = proj(wkr_ref, 0, HEAD_PAD).T
    r1, r2 = _rope_rows(krt[0:ROPE_HALF], krt[ROPE_HALF:ROPE], cos, sin)
    kr = jnp.concatenate([jnp.zeros((NOPE, tm), F32), r1, r2, jnp.zeros((HEAD_PAD - QK_DIM, tm), F32)], axis=0).T
    for h in range(N_HEADS):
        base = h * HEAD_PAD
        k_ref[0, :, base:base + HEAD_PAD] = (kpad[:, base:base + HEAD_PAD] + kr).astype(BF16)

    hc = conv_ref[...] + cb_ref[...]
    xc = hc - jnp.mean(hc, axis=-1, keepdims=True)
    ln = xc * lax.rsqrt(jnp.mean(xc * xc, axis=-1, keepdims=True) + EPS) * lng_ref[...] + lnb_ref[...]
    yconv = rows_dot((ln * jax.nn.sigmoid(ln)).astype(BF16), wco_ref[...])

    u = proj(wwide_ref, *SEG_POOL)
    pext_ref[POOL_HALO:POOL_HALO + tm, :] = u
    t1 = i * tm + 1 + lax.broadcasted_iota(jnp.int32, (tm, 1), 0)
    pool_rows = tm + POOL_HALO
    run = pext_ref[...]
    diffs = []
    for gi, w in enumerate(POOL_WINDOWS):
        run = run + pltpu.roll(run, w // 2, 0)
        tot = run[POOL_HALO:pool_rows, 0:POOL_GD]
        cnt = jnp.minimum(t1, w).astype(F32)
        diffs.append(tot / cnt - u[:, gi * POOL_GD:(gi + 1) * POOL_GD])
        run = run[:, POOL_GD:]
    pext_ref[0:POOL_HALO, :] = pext_ref[tm:tm + POOL_HALO, :]
    dmix = jnp.concatenate(diffs, axis=1).astype(BF16)
    mixed = jnp.concatenate([rows_dot(dmix[:, gi * POOL_GD:(gi + 1) * POOL_GD], pw_ref[gi])
                             for gi in range(len(POOL_WINDOWS))], axis=1) * ps_ref[...]
    ypool = rows_dot(mixed.astype(BF16), wpo_ref[...])

    part = gates_ref[:, D_MODEL:2 * D_MODEL] * yconv + gates_ref[:, 2 * D_MODEL:3 * D_MODEL] * ypool
    part_ref[0] = part.astype(BF16)
    g0_ref[0] = gates_ref[:, 0:D_MODEL].astype(BF16)


def _layer_spec(stacked, l):
    zeros = (0,) * (stacked.ndim - 1)
    return pl.BlockSpec((None,) + stacked.shape[1:], lambda b, i: (l,) + zeros, pipeline_mode=pl.Buffered(1))


def _mixer_in(x, cos_t, sin_t, p, l, *, tm):
    b, s, d = x.shape
    nt = s // tm
    row_tile = lambda w: pl.BlockSpec((1, tm, w), lambda bi, i: (bi, i, 0))
    col_tile = lambda r: pl.BlockSpec((1, r, tm), lambda bi, i: (bi, 0, i))
    assert s % tm == 0 and tm % MERGE_ROWS == 0 and tm % CONV_ROWS == 0
    consts = [p[name] for name in (
        "npre", "w_lat", "w_kr", "w_wide", "q_norm", "w_uq_t", "kv_norm", "w_uk", "w_uv_t", "conv_w", "conv_b",
        "ln_g", "ln_b", "w_conv_o", "pool_w", "pool_scale", "w_pool_o")]
    return pl.pallas_call(
        functools.partial(_mixer_in_kernel, tm=tm),
        grid=(b, nt),
        in_specs=[row_tile(d), col_tile(ROPE_HALF), col_tile(ROPE_HALF)] + [_layer_spec(c, l) for c in consts],
        out_specs=[col_tile(N_HEADS * HEAD_PAD), row_tile(N_HEADS * HEAD_PAD), col_tile(N_HEADS * V_ROWS),
                   row_tile(d), row_tile(d)],
        out_shape=[
            jax.ShapeDtypeStruct((b, N_HEADS * HEAD_PAD, s), BF16),
            jax.ShapeDtypeStruct((b, s, N_HEADS * HEAD_PAD), BF16),
            jax.ShapeDtypeStruct((b, N_HEADS * V_ROWS, s), BF16),
            jax.ShapeDtypeStruct((b, s, d), BF16),
            jax.ShapeDtypeStruct((b, s, d), BF16),
        ],
        scratch_shapes=[
            pltpu.VMEM((tm + CONV_HALO, CONV_CH), F32),
            pltpu.VMEM((SUBLANES - 1, tm + CONV_HALO - SUBLANES, CONV_CH), F32),
            pltpu.VMEM((CONV_W, SUBLANES, CONV_CH), F32),
            pltpu.VMEM((tm, CONV_CH), F32),
            pltpu.VMEM((tm + POOL_HALO, POOL_CH), F32),
            pltpu.VMEM((tm, N_BRANCHES * D_MODEL), F32),
        ],
        compiler_params=pltpu.CompilerParams(
            dimension_semantics=("arbitrary", "arbitrary"), vmem_limit_bytes=VMEM_LIMIT_BYTES),
        name="mixer_in",
    )(x, cos_t, sin_t, *consts)


def _attn_kernel(qt_ref, k_ref, vt_ref, ot_ref, s0_ref, s1_ref, m_ref, acc_ref, *, tq):
    i = pl.program_id(1)
    half = tq // 2
    m_ref[...] = jnp.full(m_ref.shape, -jnp.inf, F32)
    acc_ref[...] = jnp.zeros(acc_ref.shape, F32)

    def head(h):
        return slice(h * HEAD_PAD, (h + 1) * HEAD_PAD)

    def scores_into(s_ref, c):
        k0 = pl.multiple_of(c * tq, tq)
        for h in range(N_HEADS):
            s_ref[h] = _dot(k_ref[0, pl.ds(k0, tq), head(h)], qt_ref[0, head(h), :])

    def update(s_ref, c, k_lo, n_keys, q_lo, masked):
        k0 = pl.multiple_of(c * tq + k_lo, half)
        for h in range(N_HEADS):
            s = s_ref[h, k_lo:k_lo + n_keys, q_lo:tq]
            if masked:
                key = lax.broadcasted_iota(jnp.int32, (n_keys, 1), 0) + (k_lo - q_lo)
                qry = lax.broadcasted_iota(jnp.int32, (1, tq - q_lo), 1)
                s = jnp.where(key <= qry, s, -jnp.inf)
            m = m_ref[h, :, q_lo:tq]
            m_new = jnp.maximum(m, jnp.max(s, axis=0, keepdims=True))
            alpha = jnp.exp2(m - m_new)
            prob = jnp.exp2(s - m_new).astype(BF16)
            m_ref[h, :, q_lo:tq] = m_new
            vc = vt_ref[0, h * V_ROWS:(h + 1) * V_ROWS, pl.ds(k0, n_keys)]
            acc_ref[h, :, q_lo:tq] = alpha * acc_ref[h, :, q_lo:tq] + _dot(vc, prob)

    def diagonal(s_ref):
        update(s_ref, i, 0, half, 0, True)
        update(s_ref, i, half, half, half, True)

    def pair(p, carry):
        scores_into(s1_ref, 2 * p + 1)
        update(s0_ref, 2 * p, 0, tq, 0, False)
        scores_into(s0_ref, 2 * p + 2)
        update(s1_ref, 2 * p + 1, 0, tq, 0, False)
        return carry

    def tail_even():
        diagonal(s0_ref)

    def tail_odd():
        scores_into(s1_ref, i)
        update(s0_ref, i - 1, 0, tq, 0, False)
        diagonal(s1_ref)

    @pl.when(i == 0)
    def _():
        scores_into(s0_ref, 0)
        tail_even()

    @pl.when(i == 1)
    def _():
        scores_into(s0_ref, 0)
        tail_odd()

    @pl.when(i >= 2)
    def _():
        scores_into(s0_ref, 0)
        pair(0, 0)

    lax.fori_loop(1, i // 2, pair, 0)

    @pl.when(jnp.logical_and(i >= 2, i % 2 == 0))
    def _():
        tail_even()

    @pl.when(jnp.logical_and(i >= 2, i % 2 == 1))
    def _():
        tail_odd()

    for h in range(N_HEADS):
        ot_ref[0, h * V_DIM:(h + 1) * V_DIM, :] = (acc_ref[h, 0:V_DIM, :] / acc_ref[h, V_DIM:V_DIM + 1, :]).astype(BF16)


def _attention(q_t, k, v_t, *, tq):
    b, _, s = q_t.shape
    tk = tq
    assert s % tq == 0
    return pl.pallas_call(
        functools.partial(_attn_kernel, tq=tq),
        grid=(b, s // tq),
        in_specs=[
            pl.BlockSpec((1, N_HEADS * HEAD_PAD, tq), lambda bi, i: (bi, 0, i)),
            pl.BlockSpec((1, s, N_HEADS * HEAD_PAD), lambda bi, i: (bi, 0, 0)),
            pl.BlockSpec((1, N_HEADS * V_ROWS, s), lambda bi, i: (bi, 0, 0)),
        ],
        out_specs=pl.BlockSpec((1, N_HEADS * V_DIM, tq), lambda bi, i: (bi, 0, i)),
        out_shape=jax.ShapeDtypeStruct((b, N_HEADS * V_DIM, s), BF16),
        scratch_shapes=[
            pltpu.VMEM((N_HEADS, tk, tq), F32),
            pltpu.VMEM((N_HEADS, tk, tq), F32),
            pltpu.VMEM((N_HEADS, 1, tq), F32),
            pltpu.VMEM((N_HEADS, V_ROWS, tq), F32),
        ],
        compiler_params=pltpu.CompilerParams(
            dimension_semantics=("arbitrary", "arbitrary"), vmem_limit_bytes=VMEM_LIMIT_BYTES),
        name="attention",
    )(q_t, k, v_t)


def _merge_ffn_kernel(h_ref, ot_ref, g0_ref, part_ref, wao_ref, wmo_ref, npost_ref, fpre_ref, wg_ref, wu_ref,
                      wd_ref, fpost_ref, out_ref):
    tm = h_ref.shape[1]
    groups = [slice(r, r + MERGE_ROWS) for r in range(0, tm, MERGE_ROWS)]
    h1 = []
    for rows in groups:
        yattn = _dot(ot_ref[0, :, rows], wao_ref[...], _TN)
        merged = g0_ref[0, rows, :].astype(F32) * yattn + part_ref[0, rows, :].astype(F32)
        h1.append(h_ref[0, rows, :] + _rms(_dot(merged.astype(BF16), wmo_ref[...]), npost_ref[...]))
    act = []
    for g in range(len(groups)):
        hn = _rms(h1[g], fpre_ref[...]).astype(BF16)
        gate = _dot(hn, wg_ref[...])
        act.append((gate * jax.nn.sigmoid(gate) * _dot(hn, wu_ref[...])).astype(BF16))
    for g, rows in enumerate(groups):
        out_ref[0, rows, :] = h1[g] + _rms(_dot(act[g], wd_ref[...]), fpost_ref[...])


def _merge_ffn(h, o_t, g0, part, p, l, *, tm):
    b, s, d = h.shape
    row_tile = lambda w: pl.BlockSpec((1, tm, w), lambda bi, i: (bi, i, 0))
    assert s % tm == 0 and tm % MERGE_ROWS == 0
    consts = [p["w_attn_o"], p["w_mix_o"], p["npost"], p["fpre"], p["w_gate"], p["w_up"], p["w_down"], p["fpost"]]
    return pl.pallas_call(
        _merge_ffn_kernel,
        grid=(b, s // tm),
        in_specs=[row_tile(d), pl.BlockSpec((1, N_HEADS * V_DIM, tm), lambda bi, i: (bi, 0, i)), row_tile(d),
                  row_tile(d)] + [_layer_spec(c, l) for c in consts],
        out_specs=row_tile(d),
        out_shape=jax.ShapeDtypeStruct((b, s, d), F32),
        compiler_params=pltpu.CompilerParams(
            dimension_semantics=("arbitrary", "arbitrary"), vmem_limit_bytes=VMEM_LIMIT_BYTES),
        name="merge_ffn",
    )(h, o_t, g0, part, *consts)


def _stacked_params(mix_norm_pre, w_in, q_norm, w_uq, kv_norm, w_uk, w_uv, w_attn_o, conv_w, conv_b, conv_ln_g,
                    conv_ln_b, w_conv_o, pool_w, pool_scale, w_pool_o, w_mix_o, mix_norm_post, ffn_norm_pre, w_gate,
                    w_up, w_down, ffn_norm_post):
    depth = w_in.shape[0]
    row = lambda v: v.reshape(depth, 1, -1)
    w_in = w_in.astype(BF16)
    w_kr = jnp.pad(w_in[:, :, LAT_COLS:WIDE_START], ((0, 0), (0, 0), (0, HEAD_PAD - ROPE)))
    assert w_in.shape[2] - WIDE_START == SEG_GATES[1]

    def pad_heads(w, dh, dh_pad):
        w = jnp.pad(w.reshape(depth, w.shape[1], N_HEADS, dh), ((0, 0), (0, 0), (0, 0), (0, dh_pad - dh)))
        return w.reshape(depth, w.shape[1], N_HEADS * dh_pad)

    transpose = lambda w: jnp.swapaxes(w, 1, 2)
    return dict(
        npre=row(mix_norm_pre), w_lat=w_in[:, :, :LAT_COLS], w_kr=w_kr,
        w_wide=w_in[:, :, WIDE_START:], q_norm=row(q_norm),
        w_uq_t=transpose(pad_heads(w_uq, QK_DIM, HEAD_PAD)).astype(BF16), kv_norm=row(kv_norm),
        w_uk=pad_heads(w_uk, NOPE, HEAD_PAD).astype(BF16),
        w_uv_t=transpose(pad_heads(w_uv, V_DIM, V_ROWS)).astype(BF16),
        conv_w=conv_w, conv_b=row(conv_b), ln_g=row(conv_ln_g), ln_b=row(conv_ln_b),
        w_conv_o=w_conv_o.astype(BF16), pool_w=pool_w.astype(BF16), pool_scale=row(pool_scale),
        w_pool_o=w_pool_o.astype(BF16),
        w_attn_o=w_attn_o.astype(BF16), w_mix_o=w_mix_o.astype(BF16), npost=row(mix_norm_post),
        fpre=row(ffn_norm_pre), w_gate=w_gate.astype(BF16), w_up=w_up.astype(BF16),
        w_down=w_down.astype(BF16), fpost=row(ffn_norm_post),
    )


def kernel(x, positions, mix_norm_pre, w_in, q_norm, w_uq, kv_norm, w_uk, w_uv, w_attn_o, conv_w, conv_b, conv_ln_g, conv_ln_b, w_conv_o, pool_w, pool_scale, w_pool_o, w_mix_o, mix_norm_post, ffn_norm_pre, w_gate, w_up, w_down, ffn_norm_post):
    p = _stacked_params(mix_norm_pre, w_in, q_norm, w_uq, kv_norm, w_uk, w_uv, w_attn_o, conv_w, conv_b, conv_ln_g,
                        conv_ln_b, w_conv_o, pool_w, pool_scale, w_pool_o, w_mix_o, mix_norm_post, ffn_norm_pre,
                        w_gate, w_up, w_down, ffn_norm_post)
    assert x.dtype == F32 and x.shape[2] == D_MODEL and positions.shape == x.shape[:2]
    s = x.shape[1]
    tm = min(ROW_TILE, s)
    cos_t, sin_t = _rope_tables(positions)
    h = x
    for l in range(w_in.shape[0]):
        q_t, k, v_t, g0, part = _mixer_in(h, cos_t, sin_t, p, l, tm=tm)
        o_t = _attention(q_t, k, v_t, tq=min(ATTN_TQ, s))
        h = _merge_ffn(h, o_t, g0, part, p, l, tm=tm)
    return h
```
